```python
import jax, jax.numpy as jnp
from jax import lax
import numpy as np

D_MODEL = 1024
BATCH = 4
SEQ = 8192
DEPTH = 2

CTX_LEN = 256
GRID_W = 64
POS_BASE = 10000.0
CHUNK = 128
GMLP_GROUPS = 8
GMLP_WIDTH = 512
GMLP_GDIM = GMLP_WIDTH // GMLP_GROUPS
LRU_HEADS = 8
LRU_WIDTH = 512
LRU_HDIM = LRU_WIDTH // LRU_HEADS
CONV_W = 4
CONV_LEFT = 2
LRU_C = 8.0
SPLIT_AT = (GMLP_WIDTH, 2 * GMLP_WIDTH, 2 * GMLP_WIDTH + LRU_WIDTH,
            2 * GMLP_WIDTH + 2 * LRU_WIDTH, 2 * GMLP_WIDTH + 2 * LRU_WIDTH + D_MODEL)
IN_WIDTH = 2 * GMLP_WIDTH + 2 * LRU_WIDTH + 2 * D_MODEL
N_EXPERTS = 64
TOP_K = 8
N_GROUPS = 8
TOPK_GROUPS = 4
EXPERT_DIM = 256
SHARED_DIM = 256
ROUTE_SCALE = 2.5
MOE_BLOCK = 256
RMS_EPS = 1e-6
LN_EPS = 1e-5

kernel_name = "hybrid_gmlp_rglru_moe_prefix_dit"


def rms_norm(x, g):
    xf = x.astype(jnp.float32)
    y = xf * lax.rsqrt(jnp.mean(xf * xf, axis=-1, keepdims=True) + RMS_EPS)
    return (y * g.astype(jnp.float32)).astype(x.dtype)


def layer_norm(x, g, b):
    xf = x.astype(jnp.float32)
    mu = jnp.mean(xf, axis=-1, keepdims=True)
    var = jnp.mean(jnp.square(xf - mu), axis=-1, keepdims=True)
    y = (xf - mu) * lax.rsqrt(var + LN_EPS) * g.astype(jnp.float32) + b.astype(jnp.float32)
    return y.astype(x.dtype)


def grid_pos_embed(n_tok):
    rows = n_tok // GRID_W
    quarter = D_MODEL // 4
    half = D_MODEL // 2
    omega = 1.0 / (POS_BASE ** (jnp.arange(quarter, dtype=jnp.float32) / quarter))

    def sincos(p):
        ang = p[:, None] * omega[None, :]
        return jnp.concatenate([jnp.sin(ang), jnp.cos(ang)], axis=-1)

    row_e = sincos(jnp.arange(rows, dtype=jnp.float32))
    col_e = sincos(jnp.arange(GRID_W, dtype=jnp.float32))
    pe = jnp.concatenate([jnp.broadcast_to(row_e[:, None, :], (rows, GRID_W, half)),
                          jnp.broadcast_to(col_e[None, :, :], (rows, GRID_W, half))], axis=-1)
    return pe.reshape(rows * GRID_W, D_MODEL)


def ada_params(cond, w_mod, b_mod):
    m = jnp.einsum('bd,de->be', jax.nn.silu(cond), w_mod) + b_mod
    return [t[:, None, :] for t in jnp.split(m, 6, axis=-1)]


def mixer_inputs(h, w_in):
    z = jnp.einsum('bld,de->ble', h, w_in)
    u, v, xb, gb, ga, gr = jnp.split(z, SPLIT_AT, axis=-1)
    return (jax.nn.gelu(u), jax.nn.gelu(v), xb, jax.nn.gelu(gb),
            jax.nn.sigmoid(ga), jax.nn.sigmoid(gr))


def gmlp_spatial(u, v, ln_g, ln_b, w_s, b_s):
    bsz, n, _ = v.shape
    vn = layer_norm(v, ln_g, ln_b).reshape(bsz, n // CHUNK, CHUNK, GMLP_GROUPS, GMLP_GDIM)
    mixed = jnp.einsum('gpq,bcqgd->bcpgd', w_s, vn) + b_s.T[:, :, None]
    return u * mixed.reshape(bsz, n, GMLP_WIDTH)


def short_conv(x, w, b):
    n = x.shape[1]
    xp = jnp.pad(x, ((0, 0), (CONV_LEFT, CONV_W - 1 - CONV_LEFT), (0, 0)))
    y = b + xp[:, 0:n] * w[0]
    for k in range(1, CONV_W):
        y = y + xp[:, k:k + n] * w[k]
    return y


def lru_coeffs(xc, w_a, b_a, w_x, b_x, lam):
    bsz, n, _ = xc.shape
    xh = xc.reshape(bsz, n, LRU_HEADS, LRU_HDIM)
    r = jax.nn.sigmoid(jnp.einsum('blhi,hij->blhj', xh, w_a) + b_a).reshape(bsz, n, LRU_WIDTH)
    i = jax.nn.sigmoid(jnp.einsum('blhi,hij->blhj', xh, w_x) + b_x).reshape(bsz, n, LRU_WIDTH)
    log_a = -LRU_C * r.astype(jnp.float32) * jax.nn.softplus(-lam.astype(jnp.float32))
    a = jnp.exp(log_a)
    drive = jnp.sqrt(-jnp.expm1(2.0 * log_a)) * (i * xc).astype(jnp.float32)
    return a, drive


def _lin_combine(left, right):
    a_l, b_l = left
    a_r, b_r = right
    return a_l * a_r, a_r * b_l + b_r


def linear_scan(a, drive, h0, reverse):
    if h0 is not None:
        edge = -1 if reverse else 0
        drive = drive.at[:, edge].add(a[:, edge] * h0)
    _, h = lax.associative_scan(_lin_combine, (a, drive), reverse=reverse, axis=1)
    return h


def rglru_branch(xb_c, xb_l, conv_w, conv_b, w_a, b_a, w_x, b_x, lam):
    xc_c = short_conv(xb_c, conv_w, conv_b)
    xc_l = short_conv(xb_l, conv_w, conv_b)
    hs_c, hs_l = [], []
    for d, rev in enumerate((False, True)):
        a, drv = lru_coeffs(xc_c, w_a[d], b_a[d], w_x[d], b_x[d], lam[d])
        h_c = linear_scan(a, drv, None, rev)
        h0 = h_c[:, 0] if rev else h_c[:, -1]
        a, drv = lru_coeffs(xc_l, w_a[d], b_a[d], w_x[d], b_x[d], lam[d])
        h_l = linear_scan(a, drv, h0, rev)
        hs_c.append(h_c)
        hs_l.append(h_l)
    return (hs_c[0] + hs_c[1]).astype(xb_c.dtype), (hs_l[0] + hs_l[1]).astype(xb_l.dtype)


def merge_branches(y_a, y_b, gate_a, gate_b, w_out_a, w_out_b, w_out):
    m = (gate_a * jnp.einsum('blc,cd->bld', y_a, w_out_a)
         + gate_b * jnp.einsum('blc,cd->bld', y_b, w_out_b))
    return jnp.einsum('bld,de->ble', m, w_out)


def swiglu(x, w_g, w_u, w_d):
    return jnp.einsum('nf,fd->nd', jax.nn.silu(x @ w_g) * (x @ w_u), w_d)


def route(h, router_w, router_bias):
    t = h.shape[0]
    s = jax.nn.sigmoid(jnp.einsum('td,de->te', h, router_w).astype(jnp.float32))
    sel = s + router_bias.astype(jnp.float32)
    grp_score = lax.top_k(sel.reshape(t, N_GROUPS, N_EXPERTS // N_GROUPS), 2)[0].sum(-1)
    _, top_g = lax.top_k(grp_score, TOPK_GROUPS)
    gmask = jnp.any(top_g[:, :, None] == jnp.arange(N_GROUPS)[None, None, :], axis=1)
    emask = jnp.repeat(gmask, N_EXPERTS // N_GROUPS, axis=1)
    _, idx = lax.top_k(jnp.where(emask, sel, -jnp.inf), TOP_K)
    w = jnp.take_along_axis(s, idx, axis=1)
    w = w / jnp.sum(w, axis=-1, keepdims=True) * ROUTE_SCALE
    return idx, w


def routed_experts(h, idx, wts, w_g, w_u, w_d):
    t, d = h.shape
    n_assign = t * TOP_K
    n_rows = -(-(n_assign + N_EXPERTS * (MOE_BLOCK - 1)) // MOE_BLOCK) * MOE_BLOCK
    n_blocks = n_rows // MOE_BLOCK
    e_flat = idx.reshape(-1)
    tok_flat = jnp.arange(n_assign, dtype=jnp.int32) // TOP_K
    order = jnp.argsort(e_flat)
    e_sorted = e_flat[order]
    counts = jnp.bincount(e_flat, length=N_EXPERTS)
    padded = (counts + MOE_BLOCK - 1) // MOE_BLOCK * MOE_BLOCK
    start = jnp.cumsum(counts) - counts
    pend = jnp.cumsum(padded)
    pstart = pend - padded
    dest = pstart[e_sorted] + jnp.arange(n_assign, dtype=jnp.int32) - start[e_sorted]
    row_tok = jnp.full((n_rows,), t, jnp.int32).at[dest].set(tok_flat[order])
    row_w = jnp.zeros((n_rows,), h.dtype).at[dest].set(wts.reshape(-1)[order].astype(h.dtype))
    blk_start = jnp.arange(n_blocks, dtype=jnp.int32) * MOE_BLOCK
    blk_exp = jnp.minimum(jnp.searchsorted(pend, blk_start, side='right'), N_EXPERTS - 1)
    h_pad = jnp.concatenate([h, jnp.zeros((1, d), h.dtype)], axis=0)

    def body(out, blk):
        toks, ws, e = blk
        y = swiglu(h_pad[toks], w_g[e], w_u[e], w_d[e]) * ws[:, None]
        return out.at[toks].add(y), None

    out, _ = lax.scan(body, jnp.zeros_like(h_pad),
                      (row_tok.reshape(n_blocks, MOE_BLOCK), row_w.reshape(n_blocks, MOE_BLOCK), blk_exp))
    return out[:t]


def moe_ffn(h, router_w, router_bias, e_g, e_u, e_d, s_g, s_u, s_d):
    bsz, n, d = h.shape
    hf = h.reshape(bsz * n, d)
    idx, w = route(hf, router_w, router_bias)
    y = swiglu(hf, s_g, s_u, s_d) + routed_experts(hf, idx, w, e_g, e_u, e_d)
    return y.reshape(bsz, n, d)


def setup_inputs(seed: int = 0) -> dict:
    key = jax.random.key(seed)
    ks = iter(jax.random.split(key, 48))
    f32 = jnp.float32
    D, L = D_MODEL, DEPTH

    def nrm(shape, scale):
        return jax.random.normal(next(ks), shape, f32) * scale

    lam_u = jax.random.uniform(next(ks), (L, 2, LRU_WIDTH), f32, 0.9, 0.999)
    a_base = lam_u ** (1.0 / LRU_C)
    lru_lambda = jnp.log(a_base) - jnp.log1p(-a_base)
    return {
        "x": nrm((BATCH, SEQ, D), 1.0),
        "c": nrm((BATCH, D), 1.0),
        "ctx": nrm((BATCH, CTX_LEN, D), 1.0),
        "c_ctx": nrm((D,), 1.0),
        "w_mod": nrm((L, D, 6 * D), 0.5 * D ** -0.5),
        "b_mod": nrm((L, 6 * D), 0.02),
        "g_pre_mix": 1.0 + nrm((L, D), 0.02),
        "g_post_mix": 1.0 + nrm((L, D), 0.02),
        "g_pre_ffn": 1.0 + nrm((L, D), 0.02),
        "g_post_ffn": 1.0 + nrm((L, D), 0.02),
        "w_in": nrm((L, D, IN_WIDTH), D ** -0.5),
        "gmlp_ln_g": 1.0 + nrm((L, GMLP_WIDTH), 0.02),
        "gmlp_ln_b": nrm((L, GMLP_WIDTH), 0.02),
        "gmlp_ws": nrm((L, GMLP_GROUPS, CHUNK, CHUNK), CHUNK ** -0.5),
        "gmlp_bs": 1.0 + nrm((L, GMLP_GROUPS, CHUNK), 0.02),
        "conv_w": nrm((L, CONV_W, LRU_WIDTH), CONV_W ** -0.5),
        "conv_b": nrm((L, LRU_WIDTH), 0.02),
        "lru_wa": nrm((L, 2, LRU_HEADS, LRU_HDIM, LRU_HDIM), LRU_HDIM ** -0.5),
        "lru_ba": nrm((L, 2, LRU_HEADS, LRU_HDIM), 0.02),
        "lru_wx": nrm((L, 2, LRU_HEADS, LRU_HDIM, LRU_HDIM), LRU_HDIM ** -0.5),
        "lru_bx": nrm((L, 2, LRU_HEADS, LRU_HDIM), 0.02),
        "lru_lambda": lru_lambda,
        "w_out_a": nrm((L, GMLP_WIDTH, D), GMLP_WIDTH ** -0.5),
        "w_out_b": nrm((L, LRU_WIDTH, D), LRU_WIDTH ** -0.5),
        "w_out": nrm((L, D, D), D ** -0.5),
        "router_w": nrm((L, D, N_EXPERTS), D ** -0.5),
        "router_bias": nrm((L, N_EXPERTS), 0.01),
        "exp_w_gate": nrm((L, N_EXPERTS, D, EXPERT_DIM), D ** -0.5),
        "exp_w_up": nrm((L, N_EXPERTS, D, EXPERT_DIM), D ** -0.5),
        "exp_w_down": nrm((L, N_EXPERTS, EXPERT_DIM, D), EXPERT_DIM ** -0.5),
        "sh_w_gate": nrm((L, D, SHARED_DIM), D ** -0.5),
        "sh_w_up": nrm((L, D, SHARED_DIM), D ** -0.5),
        "sh_w_down": nrm((L, SHARED_DIM, D), SHARED_DIM ** -0.5),
    }


def reference(x, c, ctx, c_ctx, w_mod, b_mod, g_pre_mix, g_post_mix, g_pre_ffn, g_post_ffn,
              w_in, gmlp_ln_g, gmlp_ln_b, gmlp_ws, gmlp_bs, conv_w, conv_b,
              lru_wa, lru_ba, lru_wx, lru_bx, lru_lambda, w_out_a, w_out_b, w_out,
              router_w, router_bias, exp_w_gate, exp_w_up, exp_w_down,
              sh_w_gate, sh_w_up, sh_w_down):
    xl = x + grid_pos_embed(x.shape[1]).astype(x.dtype)[None]
    xc = ctx
    for l in range(DEPTH):
        last = l == DEPTH - 1
        sh_t, sc_t, gt_t, sh_f, sc_f, gt_f = ada_params(c, w_mod[l], b_mod[l])
        csh_t, csc_t, cgt_t, csh_f, csc_f, cgt_f = ada_params(c_ctx[None, :], w_mod[l], b_mod[l])

        hl = rms_norm(xl, g_pre_mix[l]) * (1.0 + sc_t) + sh_t
        hc = rms_norm(xc, g_pre_mix[l]) * (1.0 + csc_t) + csh_t
        ul, vl, xbl, gbl, gal, grl = mixer_inputs(hl, w_in[l])
        uc, vc, xbc, gbc, gac, grc = mixer_inputs(hc, w_in[l])
        lru_c, lru_l = rglru_branch(xbc, xbl, conv_w[l], conv_b[l], lru_wa[l], lru_ba[l],
                                    lru_wx[l], lru_bx[l], lru_lambda[l])
        ya_l = gmlp_spatial(ul, vl, gmlp_ln_g[l], gmlp_ln_b[l], gmlp_ws[l], gmlp_bs[l])
        yl = merge_branches(ya_l, lru_l * gbl, gal, grl, w_out_a[l], w_out_b[l], w_out[l])
        xl = xl + gt_t * rms_norm(yl, g_post_mix[l])
        if not last:
            ya_c = gmlp_spatial(uc, vc, gmlp_ln_g[l], gmlp_ln_b[l], gmlp_ws[l], gmlp_bs[l])
            yc = merge_branches(ya_c, lru_c * gbc, gac, grc, w_out_a[l], w_out_b[l], w_out[l])
            xc = xc + cgt_t * rms_norm(yc, g_post_mix[l])

        hl = rms_norm(xl, g_pre_ffn[l]) * (1.0 + sc_f) + sh_f
        fl = moe_ffn(hl, router_w[l], router_bias[l], exp_w_gate[l], exp_w_up[l], exp_w_down[l],
                     sh_w_gate[l], sh_w_up[l], sh_w_down[l])
        xl = xl + gt_f * rms_norm(fl, g_post_ffn[l])
        if not last:
            hc = rms_norm(xc, g_pre_ffn[l]) * (1.0 + csc_f) + csh_f
            fc = moe_ffn(hc, router_w[l], router_bias[l], exp_w_gate[l], exp_w_up[l], exp_w_down[l],
                         sh_w_gate[l], sh_w_up[l], sh_w_down[l])
            xc = xc + cgt_f * rms_norm(fc, g_post_ffn[l])
    return xl
```

```python
import functools

import jax
import jax.numpy as jnp
from jax import lax
from jax.experimental import pallas as pl
from jax.experimental.pallas import tpu as pltpu

F32 = jnp.float32
BF16 = jnp.bfloat16

D = 1024
BATCH = 4
SEQ = 8192
DEPTH = 2
CTX = 256
GRID_W = 64
POS_BASE = 10000.0
CHUNK = 128
GROUPS = 8
GW = 512
LW = 512
HEADS = 8
HDIM = LW // HEADS
CONV_W = 4
CONV_LEFT = 2
LRU_C = 8.0
IN_WIDTH = 2 * GW + 2 * LW + 2 * D
N_EXP = 64
TOP_K = 8
N_GRP = 8
TOPK_GRP = 4
EDIM = 256
SDIM = 256
ROUTE_SCALE = 2.5
RMS_EPS = 1e-6
LN_EPS = 1e-5

SUBLANES = 8
LANES = 128
SLABS = D // LANES
VMEM_LIMIT = 56 * 1024 * 1024

N_CTX = BATCH * CTX
NT = N_CTX + BATCH * SEQ
TM = 512
N_TILES = NT // TM
CTX_TILES = N_CTX // TM
TILES_PER_SEQ = SEQ // TM
TL = CTX
LRU_STEPS = 1 + SEQ // TL
MOE_BLOCK = 256
N_ASSIGN = NT * TOP_K
N_BLOCKS = -(-(N_ASSIGN + N_EXP * (MOE_BLOCK - 1)) // MOE_BLOCK)
N_ROWS = N_BLOCKS * MOE_BLOCK
N_COND = 8


def _cparams(sem):
    return pltpu.CompilerParams(dimension_semantics=sem, vmem_limit_bytes=VMEM_LIMIT)


def _mod_row(i):
    return jnp.where(i < CTX_TILES, 0, 1 + (i - CTX_TILES) // TILES_PER_SEQ)


def _rms(x, g):
    return x * lax.rsqrt(jnp.mean(x * x, axis=-1, keepdims=True) + RMS_EPS) * g


def _to_slabs(ref, val):
    rows = val.shape[0]
    for s in range(SLABS):
        ref[pl.ds(s, rows, stride=SLABS), :] = val[:, s * LANES:(s + 1) * LANES]


def _from_slabs(ref, rows):
    return jnp.concatenate([ref[pl.ds(s, rows, stride=SLABS), :] for s in range(SLABS)], axis=1)


def _ada_kernel(c_ref, w_ref, b_ref, o_ref):
    a = jax.nn.silu(c_ref[...])
    o_ref[0, 0] = jnp.dot(a, w_ref[0], precision=lax.Precision.HIGHEST,
                          preferred_element_type=F32) + b_ref[0, 0]


def _ada(cond, w_mod, b_mod):
    out = pl.pallas_call(
        _ada_kernel,
        grid=(DEPTH, 6),
        in_specs=[pl.BlockSpec((N_COND, D), lambda l, j: (0, 0)),
                  pl.BlockSpec((1, D, D), lambda l, j: (l, 0, j)),
                  pl.BlockSpec((1, 1, 1, D), lambda l, j: (l, j, 0, 0))],
        out_specs=pl.BlockSpec((1, 1, N_COND, D), lambda l, j: (l, j, 0, 0)),
        out_shape=jax.ShapeDtypeStruct((DEPTH, 6, N_COND, D), F32),
        compiler_params=_cparams(("parallel", "parallel")),
        name="ada_params",
    )(cond, w_mod, b_mod.reshape(DEPTH, 6, 1, D))
    mods = jnp.transpose(out, (0, 2, 1, 3))
    return jnp.pad(mods, ((0, 0), (0, 0), (0, 2), (0, 0)))


def _entry_kernel(ctx_ref, x_ref, pe_ref, o_ref):
    i = pl.program_id(0)

    @pl.when(i < CTX_TILES)
    def _():
        o_ref[...] = ctx_ref[...]

    @pl.when(i >= CTX_TILES)
    def _():
        o_ref[...] = x_ref[...] + pe_ref[...]


def _entry(ctx2, x2, pe):
    return pl.pallas_call(
        _entry_kernel,
        grid=(N_TILES,),
        in_specs=[pl.BlockSpec((TM, D), lambda i: (jnp.minimum(i, CTX_TILES - 1), 0)),
                  pl.BlockSpec((TM, D), lambda i: (jnp.maximum(i - CTX_TILES, 0), 0)),
                  pl.BlockSpec((TM, D), lambda i: (jnp.maximum(i - CTX_TILES, 0) % TILES_PER_SEQ, 0))],
        out_specs=pl.BlockSpec((TM, D), lambda i: (i, 0)),
        out_shape=jax.ShapeDtypeStruct((NT, D), F32),
        compiler_params=_cparams(("parallel",)),
        name="entry",
    )(ctx2, x2, pe)


def _grid_pos_embed():
    rows = SEQ // GRID_W
    quarter = D // 4
    half = D // 2
    omega = 1.0 / (POS_BASE ** (jnp.arange(quarter, dtype=F32) / quarter))

    def sincos(p):
        ang = p[:, None] * omega[None, :]
        return jnp.concatenate([jnp.sin(ang), jnp.cos(ang)], axis=-1)

    row_e = sincos(jnp.arange(rows, dtype=F32))
    col_e = sincos(jnp.arange(GRID_W, dtype=F32))
    pe = jnp.concatenate([jnp.broadcast_to(row_e[:, None, :], (rows, GRID_W, half)),
                          jnp.broadcast_to(col_e[None, :, :], (rows, GRID_W, half))], axis=-1)
    return pe.reshape(SEQ, D)


def _premix_kernel(x_ref, m_ref, g_ref, w_ref, u_ref, v_ref, xb_ref, gb_ref, ga_ref, gr_ref):
    x = x_ref[...]
    h = _rms(x, g_ref[...]) * (1.0 + m_ref[0, 1:2, :]) + m_ref[0, 0:1, :]
    hb = h.astype(BF16)

    def proj(lo, width):
        return jnp.dot(hb, w_ref[:, lo:lo + width], preferred_element_type=F32)

    u_ref[...] = jax.nn.gelu(proj(0, GW)).astype(BF16)
    v_ref[...] = jax.nn.gelu(proj(GW, GW)).astype(BF16)
    xb_ref[...] = proj(2 * GW, LW)
    gb_ref[...] = jax.nn.gelu(proj(2 * GW + LW, LW)).astype(BF16)
    base = 2 * GW + 2 * LW
    for j in range(2):
        ga_ref[:, j * 512:(j + 1) * 512] = jax.nn.sigmoid(proj(base + j * 512, 512)).astype(BF16)
        gr_ref[:, j * 512:(j + 1) * 512] = jax.nn.sigmoid(proj(base + D + j * 512, 512)).astype(BF16)


def _premix(xtok, mods, g_pre, w_in_bf):
    tile = lambda w: pl.BlockSpec((TM, w), lambda i: (i, 0))
    return pl.pallas_call(
        _premix_kernel,
        grid=(N_TILES,),
        in_specs=[tile(D),
                  pl.BlockSpec((1, 8, D), lambda i: (_mod_row(i), 0, 0)),
                  pl.BlockSpec((1, D), lambda i: (0, 0)),
                  pl.BlockSpec((D, IN_WIDTH), lambda i: (0, 0))],
        out_specs=[tile(GW), tile(GW), tile(LW), tile(LW), tile(D), tile(D)],
        out_shape=[jax.ShapeDtypeStruct((NT, GW), BF16), jax.ShapeDtypeStruct((NT, GW), BF16),
                   jax.ShapeDtypeStruct((NT, LW), F32), jax.ShapeDtypeStruct((NT, LW), BF16),
                   jax.ShapeDtypeStruct((NT, D), BF16), jax.ShapeDtypeStruct((NT, D), BF16)],
        compiler_params=_cparams(("parallel",)),
        name="premix",
    )(xtok, mods, g_pre, w_in_bf)


def _lru_block(b, j, reverse):
    k = (LRU_STEPS - 1 - j) if reverse else (j - 1)
    return jnp.where(j == 0, b, BATCH + b * (SEQ // TL) + k), k


def _lru_kernel(x_ref, xp_ref, xn_ref, cw_ref, cb_ref, wg_ref, bg_ref, lam_ref, h_ref,
                xpad, a_buf, d_buf, carry, *, reverse):
    j = pl.program_id(1)
    k = (LRU_STEPS - 1 - j) if reverse else (j - 1)
    has_prev = jnp.logical_and(j > 0, k >= 1)
    has_next = jnp.logical_and(j > 0, k <= SEQ // TL - 2)

    @pl.when(j == 0)
    def _():
        carry[...] = jnp.zeros_like(carry)

    xpad[0:SUBLANES, :] = jnp.where(has_prev, xp_ref[...], 0.0)
    xpad[SUBLANES:SUBLANES + TL, :] = x_ref[...]
    xpad[SUBLANES + TL:, :] = jnp.where(has_next, xn_ref[...], 0.0)
    xc = cb_ref[...]
    for tap in range(CONV_W):
        off = SUBLANES - CONV_LEFT + tap
        xc = xc + xpad[off:off + TL, :] * cw_ref[tap:tap + 1, :]

    gates = jnp.dot(xc.astype(BF16), wg_ref[0], preferred_element_type=F32) + bg_ref[0]
    r = jax.nn.sigmoid(gates[:, :LW])
    ig = jax.nn.sigmoid(gates[:, LW:])
    lam = lam_ref[0]
    softplus_neg = jnp.maximum(-lam, 0.0) + jnp.log1p(jnp.exp(-jnp.abs(lam)))
    log_a = -LRU_C * r * softplus_neg
    a_buf[...] = jnp.exp(log_a)
    th = jnp.tanh(log_a)
    d_buf[...] = jnp.sqrt(-2.0 * th / (1.0 - th)) * (ig * xc)

    row = lax.broadcasted_iota(jnp.int32, (SUBLANES, LW), 0)
    n_grp = TL // SUBLANES

    def body(g, c):
        gi = (n_grp - 1 - g) if reverse else g
        sl = pl.ds(pl.multiple_of(gi * SUBLANES, SUBLANES), SUBLANES)
        a = a_buf[sl, :]
        d = d_buf[sl, :]
        for s in (1, 2, 4):
            shift = (SUBLANES - s) if reverse else s
            m = (row < SUBLANES - s) if reverse else (row >= s)
            d = jnp.where(m, a * pltpu.roll(d, shift, 0) + d, d)
            a = jnp.where(m, a * pltpu.roll(a, shift, 0), a)
        h = d + a * c
        h_ref[sl, :] = h
        edge = 0 if reverse else SUBLANES - 1
        return jnp.broadcast_to(h[edge:edge + 1, :], (SUBLANES, LW))

    carry[...] = lax.fori_loop(0, n_grp, body, carry[...], unroll=4)


def _lru(xb, conv_w, conv_b, wg, bg, lam, reverse):
    n8 = NT // SUBLANES
    per8 = TL // SUBLANES

    def cur(b, j):
        return (_lru_block(b, j, reverse)[0], 0)

    def prev(b, j):
        return (jnp.maximum(_lru_block(b, j, reverse)[0] * per8 - 1, 0), 0)

    def nxt(b, j):
        return (jnp.minimum(_lru_block(b, j, reverse)[0] * per8 + per8, n8 - 1), 0)

    const2 = lambda shape: pl.BlockSpec(shape, lambda b, j: (0, 0))
    const3 = lambda shape: pl.BlockSpec(shape, lambda b, j: (0, 0, 0))
    return pl.pallas_call(
        functools.partial(_lru_kernel, reverse=reverse),
        grid=(BATCH, LRU_STEPS),
        in_specs=[pl.BlockSpec((TL, LW), cur),
                  pl.BlockSpec((SUBLANES, LW), prev),
                  pl.BlockSpec((SUBLANES, LW), nxt),
                  const2((CONV_W, LW)), const2((1, LW)),
                  const3((1, LW, 2 * LW)), const3((1, 1, 2 * LW)), const3((1, 1, LW))],
        out_specs=pl.BlockSpec((TL, LW), cur),
        out_shape=jax.ShapeDtypeStruct((NT, LW), F32),
        scratch_shapes=[pltpu.VMEM((TL + 2 * SUBLANES, LW), F32),
                        pltpu.VMEM((TL, LW), F32), pltpu.VMEM((TL, LW), F32),
                        pltpu.VMEM((SUBLANES, LW), F32)],
        compiler_params=_cparams(("arbitrary", "arbitrary")),
        name="lru_bwd" if reverse else "lru_fwd",
    )(xb, xb, xb, conv_w, conv_b, wg, bg, lam)


def _block_diag(w):
    eye = jnp.eye(HEADS, dtype=w.dtype)
    return jnp.einsum('hij,hg->higj', w, eye).reshape(LW, LW)


def _postmix_kernel(x_ref, m_ref, u_ref, v_ref, gb_ref, ga_ref, gr_ref, hf_ref, hb_ref,
                    lng_ref, lnb_ref, ws_ref, bs_ref, woa_ref, wob_ref, wo_ref, gp_ref, o_ref,
                    mix_buf):
    v = v_ref[...].astype(F32)
    mu = jnp.mean(v, axis=-1, keepdims=True)
    var = jnp.mean(jnp.square(v - mu), axis=-1, keepdims=True)
    vn = ((v - mu) * lax.rsqrt(var + LN_EPS) * lng_ref[...] + lnb_ref[...]).astype(BF16)
    col_grp = lax.broadcasted_iota(jnp.int32, (CHUNK, GW), 1) // (GW // GROUPS)
    for c in range(TM // CHUNK):
        vc = vn[c * CHUNK:(c + 1) * CHUNK, :]
        stacked = jnp.concatenate(
            [jnp.where(col_grp == g, vc, jnp.zeros_like(vc)) for g in range(GROUPS)], axis=0)
        mix_buf[c * CHUNK:(c + 1) * CHUNK, :] = (
            jnp.dot(ws_ref[...], stacked, preferred_element_type=F32) + bs_ref[...])
    ya = (u_ref[...].astype(F32) * mix_buf[...]).astype(BF16)
    za = jnp.dot(ya, woa_ref[...], preferred_element_type=F32)
    yb = ((hf_ref[...] + hb_ref[...]) * gb_ref[...].astype(F32)).astype(BF16)
    zb = jnp.dot(yb, wob_ref[...], preferred_element_type=F32)
    m = (ga_ref[...].astype(F32) * za + gr_ref[...].astype(F32) * zb).astype(BF16)
    y = jnp.dot(m, wo_ref[...], preferred_element_type=F32)
    o_ref[...] = x_ref[...] + m_ref[0, 2:3, :] * _rms(y, gp_ref[...])


def _postmix(xtok, mods, u, v, gb, ga, gr, hf, hb, ln_g, ln_b, ws_cat, bs_full, woa, wob, wo, g_post):
    tile = lambda w: pl.BlockSpec((TM, w), lambda i: (i, 0))
    const = lambda shape: pl.BlockSpec(shape, lambda i: (0, 0))
    return pl.pallas_call(
        _postmix_kernel,
        grid=(N_TILES,),
        in_specs=[tile(D), pl.BlockSpec((1, 8, D), lambda i: (_mod_row(i), 0, 0)),
                  tile(GW), tile(GW), tile(LW), tile(D), tile(D), tile(LW), tile(LW),
                  const((1, GW)), const((1, GW)), const((CHUNK, GROUPS * CHUNK)), const((CHUNK, GW)),
                  const((GW, D)), const((LW, D)), const((D, D)), const((1, D))],
        out_specs=tile(D),
        out_shape=jax.ShapeDtypeStruct((NT, D), F32),
        scratch_shapes=[pltpu.VMEM((TM, GW), F32)],
        compiler_params=_cparams(("parallel",)),
        name="postmix",
    )(xtok, mods, u, v, gb, ga, gr, hf, hb, ln_g, ln_b, ws_cat, bs_full, woa, wob, wo, g_post)


def _route_kernel(x_ref, m_ref, g_ref, rw_ref, rb_ref, tri_ref,
                  h_ref, se_ref, sp_ref, sw_ref, cnt_ref, cnt):
    i = pl.program_id(0)

    @pl.when(i == 0)
    def _():
        cnt[...] = jnp.zeros_like(cnt)

    h = _rms(x_ref[...], g_ref[...]) * (1.0 + m_ref[0, 4:5, :]) + m_ref[0, 3:4, :]
    _to_slabs(h_ref, h)

    logits = lax.dot_general(rw_ref[...], h, (((1,), (1,)), ((), ())),
                             precision=lax.Precision.HIGHEST, preferred_element_type=F32)
    s = jax.nn.sigmoid(logits)
    sel = s + rb_ref[...]
    per_grp = N_EXP // N_GRP
    sel3 = sel.reshape(N_GRP, per_grp, TM)
    m1 = jnp.max(sel3, axis=1)
    is_m1 = sel3 == m1[:, None, :]
    n_m1 = jnp.sum(is_m1.astype(F32), axis=1)
    m2 = jnp.max(jnp.where(is_m1, -jnp.inf, sel3), axis=1)
    grp_score = m1 + jnp.where(n_m1 >= 2.0, m1, m2)

    g_iota = lax.broadcasted_iota(jnp.int32, (N_GRP, TM), 0)
    g_rank = jnp.zeros((N_GRP, TM), F32)
    for g in range(N_GRP):
        other = grp_score[g:g + 1, :]
        g_rank = g_rank + jnp.where(other > grp_score, 1.0, 0.0)
        g_rank = g_rank + jnp.where(jnp.logical_and(other == grp_score, g_iota > g), 1.0, 0.0)
    g_keep = jnp.where(g_rank < TOPK_GRP, 1.0, 0.0)
    e_keep = jnp.broadcast_to(g_keep[:, None, :], (N_GRP, per_grp, TM)).reshape(N_EXP, TM)
    selm = jnp.where(e_keep > 0.0, sel, -jnp.inf)

    e_iota = lax.broadcasted_iota(jnp.int32, (N_EXP, TM), 0)
    e_rank = jnp.zeros((N_EXP, TM), F32)
    for e in range(N_EXP):
        other = selm[e:e + 1, :]
        e_rank = e_rank + jnp.where(other > selm, 1.0, 0.0)
        e_rank = e_rank + jnp.where(jnp.logical_and(other == selm, e_iota > e), 1.0, 0.0)
    chosen = e_rank < TOP_K
    sc = jnp.where(chosen, s, 0.0)
    w = sc / jnp.sum(sc, axis=0, keepdims=True) * ROUTE_SCALE

    cum = jnp.dot(jnp.where(chosen, 1.0, 0.0).astype(BF16), tri_ref[...],
                  preferred_element_type=F32)
    pos = cnt[:, 0:1] + cum - 1.0
    cnt[...] = cnt[...] + cum[:, TM - 1:TM]
    cnt_ref[...] = cnt[...]

    e_f = e_iota.astype(F32)
    for r in range(TOP_K):
        hit = e_rank == float(r)
        se_ref[r:r + 1, :] = jnp.sum(jnp.where(hit, e_f, 0.0), axis=0, keepdims=True).astype(jnp.int32)
        sp_ref[r:r + 1, :] = jnp.sum(jnp.where(hit, pos, 0.0), axis=0, keepdims=True).astype(jnp.int32)
        sw_ref[r:r + 1, :] = jnp.sum(jnp.where(hit, w, 0.0), axis=0, keepdims=True)


def _route(xtok, mods, g_pre, rw_t, rb, tri):
    const = lambda shape: pl.BlockSpec(shape, lambda i: (0, 0))
    slot = pl.BlockSpec((TOP_K, TM), lambda i: (0, i))
    return pl.pallas_call(
        _route_kernel,
        grid=(N_TILES,),
        in_specs=[pl.BlockSpec((TM, D), lambda i: (i, 0)),
                  pl.BlockSpec((1, 8, D), lambda i: (_mod_row(i), 0, 0)),
                  const((1, D)), const((N_EXP, D)), const((N_EXP, 1)), const((TM, TM))],
        out_specs=[pl.BlockSpec((TM * SLABS, LANES), lambda i: (i, 0)), slot, slot, slot,
                   const((N_EXP, LANES))],
        out_shape=[jax.ShapeDtypeStruct((NT * SLABS, LANES), F32),
                   jax.ShapeDtypeStruct((TOP_K, NT), jnp.int32),
                   jax.ShapeDtypeStruct((TOP_K, NT), jnp.int32),
                   jax.ShapeDtypeStruct((TOP_K, NT), F32),
                   jax.ShapeDtypeStruct((N_EXP, LANES), F32)],
        scratch_shapes=[pltpu.VMEM((N_EXP, LANES), F32)],
        compiler_params=_cparams(("arbitrary",)),
        name="route",
    )(xtok, mods, g_pre, rw_t, rb, tri)


def _dispatch_kernel(dest_ref, h_ref, xs_ref, sem):
    def row_copy(t, k):
        return pltpu.make_async_copy(h_ref.at[t], xs_ref.at[dest_ref[k, t]], sem)

    def issue(t, carry):
        for k in range(TOP_K):
            row_copy(t, k).start()
        return carry

    lax.fori_loop(0, TM, issue, 0)

    def drain(t, carry):
        for k in range(TOP_K):
            row_copy(t, k).wait()
        return carry

    lax.fori_loop(0, TM, drain, 0)


def _dispatch(dest, h3):
    return pl.pallas_call(
        _dispatch_kernel,
        grid=(N_TILES,),
        in_specs=[pl.BlockSpec((TOP_K, TM), lambda i: (0, i), memory_space=pltpu.SMEM),
                  pl.BlockSpec((TM, SLABS, LANES), lambda i: (i, 0, 0))],
        out_specs=pl.BlockSpec(memory_space=pl.ANY),
        out_shape=jax.ShapeDtypeStruct((N_ROWS, SLABS, LANES), F32),
        scratch_shapes=[pltpu.SemaphoreType.DMA],
        compiler_params=_cparams(("arbitrary",)),
        name="dispatch",
    )(dest, h3)


def _expert_kernel(be_ref, bv_ref, nu_ref, x_ref, wgu_ref, wd_ref, y_ref):
    b = pl.program_id(0)

    @pl.when(b < nu_ref[0])
    def _():
        x = _from_slabs(x_ref, MOE_BLOCK)
        row = lax.broadcasted_iota(jnp.int32, (MOE_BLOCK, D), 0)
        x = jnp.where(row < bv_ref[b], x, 0.0).astype(BF16)
        gu = jnp.dot(x, wgu_ref[0], preferred_element_type=F32)
        a = (jax.nn.silu(gu[:, :EDIM]) * gu[:, EDIM:]).astype(BF16)
        _to_slabs(y_ref, jnp.dot(a, wd_ref[0], preferred_element_type=F32))


def _experts(blk_exp, blk_valid, n_used, xs2, wgu, wd):
    def rows(b, be, bv, nu):
        return (jnp.minimum(b, nu[0] - 1), 0)

    def wsel(b, be, bv, nu):
        return (be[jnp.minimum(b, nu[0] - 1)], 0, 0)

    grid_spec = pltpu.PrefetchScalarGridSpec(
        num_scalar_prefetch=3,
        grid=(N_BLOCKS,),
        in_specs=[pl.BlockSpec((MOE_BLOCK * SLABS, LANES), rows),
                  pl.BlockSpec((1, D, 2 * EDIM), wsel),
                  pl.BlockSpec((1, EDIM, D), wsel)],
        out_specs=pl.BlockSpec((MOE_BLOCK * SLABS, LANES), rows),
    )
    return pl.pallas_call(
        _expert_kernel,
        grid_spec=grid_spec,
        out_shape=jax.ShapeDtypeStruct((N_ROWS * SLABS, LANES), F32),
        compiler_params=_cparams(("arbitrary",)),
        name="experts",
    )(blk_exp, blk_valid, n_used, xs2, wgu, wd)


def _combine_kernel(dest_ref, sw_ref, x_ref, m_ref, h_ref, ys_ref, sgu_ref, sd_ref, gp_ref, o_ref,
                    ybuf, moe_buf, sem):
    def row_copy(t, k):
        return pltpu.make_async_copy(ys_ref.at[dest_ref[k, t]], ybuf.at[k, t], sem)

    def issue(t, carry):
        for k in range(TOP_K):
            row_copy(t, k).start()
        return carry

    lax.fori_loop(0, TM, issue, 0)

    hb = _from_slabs(h_ref, TM).astype(BF16)
    gu = jnp.dot(hb, sgu_ref[...], preferred_element_type=F32)
    a = (jax.nn.silu(gu[:, :SDIM]) * gu[:, SDIM:]).astype(BF16)
    shared = jnp.dot(a, sd_ref[...], preferred_element_type=F32)

    def drain(t, carry):
        for k in range(TOP_K):
            row_copy(t, k).wait()
        return carry

    lax.fori_loop(0, TM, drain, 0)

    def mix(t, carry):
        acc = sw_ref[0, t] * ybuf[0, t]
        for k in range(1, TOP_K):
            acc = acc + sw_ref[k, t] * ybuf[k, t]
        moe_buf[pl.ds(pl.multiple_of(t * SLABS, SLABS), SLABS), :] = acc
        return carry

    lax.fori_loop(0, TM, mix, 0)
    f = shared + _from_slabs(moe_buf, TM)
    o_ref[...] = x_ref[...] + m_ref[0, 5:6, :] * _rms(f, gp_ref[...])


def _combine(dest, sw, xtok, mods, h2, ys3, sgu, sd, g_post):
    const = lambda shape: pl.BlockSpec(shape, lambda i: (0, 0))
    smem = lambda: pl.BlockSpec((TOP_K, TM), lambda i: (0, i), memory_space=pltpu.SMEM)
    return pl.pallas_call(
        _combine_kernel,
        grid=(N_TILES,),
        in_specs=[smem(), smem(),
                  pl.BlockSpec((TM, D), lambda i: (i, 0)),
                  pl.BlockSpec((1, 8, D), lambda i: (_mod_row(i), 0, 0)),
                  pl.BlockSpec((TM * SLABS, LANES), lambda i: (i, 0)),
                  pl.BlockSpec(memory_space=pl.ANY),
                  const((D, 2 * SDIM)), const((SDIM, D)), const((1, D))],
        out_specs=pl.BlockSpec((TM, D), lambda i: (i, 0)),
        out_shape=jax.ShapeDtypeStruct((NT, D), F32),
        scratch_shapes=[pltpu.VMEM((TOP_K, TM, SLABS, LANES), F32),
                        pltpu.VMEM((TM * SLABS, LANES), F32),
                        pltpu.SemaphoreType.DMA],
        compiler_params=_cparams(("arbitrary",)),
        name="combine",
    )(dest, sw, xtok, mods, h2, ys3, sgu, sd, g_post)


def _moe(xtok, mods, g_pre, g_post, router_w, router_bias, e_g, e_u, e_d, s_g, s_u, s_d, tri):
    h2, slot_e, slot_p, slot_w, cnt = _route(
        xtok, mods, g_pre, router_w.T, router_bias.reshape(N_EXP, 1), tri)
    counts = cnt[:, 0].astype(jnp.int32)
    padded = (counts + MOE_BLOCK - 1) // MOE_BLOCK * MOE_BLOCK
    pend = jnp.cumsum(padded)
    pstart = pend - padded
    dest = pstart[slot_e] + slot_p
    blk_start = jnp.arange(N_BLOCKS, dtype=jnp.int32) * MOE_BLOCK
    blk_exp = jnp.minimum(jnp.searchsorted(pend, blk_start, side='right'), N_EXP - 1).astype(jnp.int32)
    blk_valid = jnp.clip(counts[blk_exp] - (blk_start - pstart[blk_exp]), 0, MOE_BLOCK).astype(jnp.int32)
    n_used = (pend[-1:] // MOE_BLOCK).astype(jnp.int32)

    xs3 = _dispatch(dest, h2.reshape(NT, SLABS, LANES))
    wgu = jnp.concatenate([e_g, e_u], axis=-1).astype(BF16)
    ys2 = _experts(blk_exp, blk_valid, n_used, xs3.reshape(N_ROWS * SLABS, LANES), wgu, e_d.astype(BF16))
    sgu = jnp.concatenate([s_g, s_u], axis=-1).astype(BF16)
    return _combine(dest, slot_w, xtok, mods, h2, ys2.reshape(N_ROWS, SLABS, LANES),
                    sgu, s_d.astype(BF16), g_post)


def kernel(x, c, ctx, c_ctx, w_mod, b_mod, g_pre_mix, g_post_mix, g_pre_ffn, g_post_ffn, w_in, gmlp_ln_g, gmlp_ln_b, gmlp_ws, gmlp_bs, conv_w, conv_b, lru_wa, lru_ba, lru_wx, lru_bx, lru_lambda, w_out_a, w_out_b, w_out, router_w, router_bias, exp_w_gate, exp_w_up, exp_w_down, sh_w_gate, sh_w_up, sh_w_down):
    cond = jnp.concatenate([c_ctx[None, :], c, jnp.zeros((N_COND - 1 - BATCH, D), F32)], axis=0)
    mods_all = _ada(cond, w_mod, b_mod)
    xtok = _entry(ctx.reshape(N_CTX, D), x.reshape(BATCH * SEQ, D), _grid_pos_embed())
    tri = (jnp.arange(TM)[:, None] <= jnp.arange(TM)[None, :]).astype(BF16)

    for l in range(DEPTH):
        mods = mods_all[l]
        row = lambda p: p[l].reshape(1, -1)
        u, v, xb, gb, ga, gr = _premix(xtok, mods, row(g_pre_mix), w_in[l].astype(BF16))

        hs = []
        for d in range(2):
            wg = jnp.concatenate([_block_diag(lru_wa[l, d]), _block_diag(lru_wx[l, d])], axis=1)
            bg = jnp.concatenate([lru_ba[l, d].reshape(1, LW), lru_bx[l, d].reshape(1, LW)], axis=1)
            hs.append(_lru(xb, conv_w[l], conv_b[l].reshape(1, LW), wg.astype(BF16)[None],
                           bg[None], lru_lambda[l, d].reshape(1, 1, LW), reverse=(d == 1)))

        ws_cat = jnp.transpose(gmlp_ws[l], (1, 0, 2)).reshape(CHUNK, GROUPS * CHUNK).astype(BF16)
        bs_full = jnp.repeat(gmlp_bs[l].T, GW // GROUPS, axis=1)
        xtok = _postmix(xtok, mods, u, v, gb, ga, gr, hs[0], hs[1], row(gmlp_ln_g), row(gmlp_ln_b),
                        ws_cat, bs_full, w_out_a[l].astype(BF16), w_out_b[l].astype(BF16),
                        w_out[l].astype(BF16), row(g_post_mix))

        xtok = _moe(xtok, mods, row(g_pre_ffn), row(g_post_ffn), router_w[l], router_bias[l],
                    exp_w_gate[l], exp_w_up[l], exp_w_down[l],
                    sh_w_gate[l], sh_w_up[l], sh_w_down[l], tri)

    return xtok[N_CTX:].reshape(BATCH, SEQ, D)
```

```python
import functools

import jax
import jax.numpy as jnp
from jax import lax
from jax.experimental import pallas as pl
from jax.experimental.pallas import tpu as pltpu

F32 = jnp.float32
BF16 = jnp.bfloat16

D = 1024
BATCH = 4
SEQ = 8192
DEPTH = 2
CTX = 256
GRID_W = 64
POS_BASE = 10000.0
CHUNK = 128
GROUPS = 8
GW = 512
LW = 512
HEADS = 8
HDIM = LW // HEADS
CONV_W = 4
CONV_LEFT = 2
LRU_C = 8.0
IN_WIDTH = 2 * GW + 2 * LW + 2 * D
N_EXP = 64
TOP_K = 8
N_GRP = 8
TOPK_GRP = 4
EDIM = 256
SDIM = 256
ROUTE_SCALE = 2.5
RMS_EPS = 1e-6
LN_EPS = 1e-5

SUBLANES = 8
LANES = 128
SLABS = D // LANES
VMEM_LIMIT = 56 * 1024 * 1024

N_CTX = BATCH * CTX
NT = N_CTX + BATCH * SEQ
TM = 512
N_TILES = NT // TM
CTX_TILES = N_CTX // TM
TILES_PER_SEQ = SEQ // TM
TL = CTX
LRU_STEPS = 1 + SEQ // TL
MOE_BLOCK = 256
N_ROWS = NT * TOP_K
N_BLOCKS = N_ROWS // MOE_BLOCK
N_ITEMS = N_BLOCKS + N_EXP
N_COND = 8


def _cparams(sem):
    return pltpu.CompilerParams(dimension_semantics=sem, vmem_limit_bytes=VMEM_LIMIT)


def _mod_row(i):
    return jnp.where(i < CTX_TILES, 0, 1 + (i - CTX_TILES) // TILES_PER_SEQ)


def _rms(x, g):
    return x * lax.rsqrt(jnp.mean(x * x, axis=-1, keepdims=True) + RMS_EPS) * g


def _to_slabs(ref, val):
    rows = val.shape[0]
    for s in range(SLABS):
        ref[pl.ds(s, rows, stride=SLABS), :] = val[:, s * LANES:(s + 1) * LANES]


def _from_slabs(ref, rows):
    return jnp.concatenate([ref[pl.ds(s, rows, stride=SLABS), :] for s in range(SLABS)], axis=1)


def _ada_kernel(c_ref, w_ref, b_ref, o_ref):
    a = jax.nn.silu(c_ref[...])
    o_ref[0, 0] = jnp.dot(a, w_ref[0], precision=lax.Precision.HIGHEST,
                          preferred_element_type=F32) + b_ref[0, 0]


def _ada(cond, w_mod, b_mod):
    out = pl.pallas_call(
        _ada_kernel,
        grid=(DEPTH, 6),
        in_specs=[pl.BlockSpec((N_COND, D), lambda l, j: (0, 0)),
                  pl.BlockSpec((1, D, D), lambda l, j: (l, 0, j)),
                  pl.BlockSpec((1, 1, 1, D), lambda l, j: (l, j, 0, 0))],
        out_specs=pl.BlockSpec((1, 1, N_COND, D), lambda l, j: (l, j, 0, 0)),
        out_shape=jax.ShapeDtypeStruct((DEPTH, 6, N_COND, D), F32),
        compiler_params=_cparams(("parallel", "parallel")),
        name="ada_params",
    )(cond, w_mod, b_mod.reshape(DEPTH, 6, 1, D))
    mods = jnp.transpose(out, (0, 2, 1, 3))
    return jnp.pad(mods, ((0, 0), (0, 0), (0, 2), (0, 0)))


def _entry_kernel(ctx_ref, x_ref, pe_ref, o_ref):
    i = pl.program_id(0)

    @pl.when(i < CTX_TILES)
    def _():
        o_ref[...] = ctx_ref[...]

    @pl.when(i >= CTX_TILES)
    def _():
        o_ref[...] = x_ref[...] + pe_ref[...]


def _entry(ctx2, x2, pe):
    return pl.pallas_call(
        _entry_kernel,
        grid=(N_TILES,),
        in_specs=[pl.BlockSpec((TM, D), lambda i: (jnp.minimum(i, CTX_TILES - 1), 0)),
                  pl.BlockSpec((TM, D), lambda i: (jnp.maximum(i - CTX_TILES, 0), 0)),
                  pl.BlockSpec((TM, D), lambda i: (jnp.maximum(i - CTX_TILES, 0) % TILES_PER_SEQ, 0))],
        out_specs=pl.BlockSpec((TM, D), lambda i: (i, 0)),
        out_shape=jax.ShapeDtypeStruct((NT, D), F32),
        compiler_params=_cparams(("parallel",)),
        name="entry",
    )(ctx2, x2, pe)


def _grid_pos_embed():
    rows = SEQ // GRID_W
    quarter = D // 4
    half = D // 2
    omega = 1.0 / (POS_BASE ** (jnp.arange(quarter, dtype=F32) / quarter))

    def sincos(p):
        ang = p[:, None] * omega[None, :]
        return jnp.concatenate([jnp.sin(ang), jnp.cos(ang)], axis=-1)

    row_e = sincos(jnp.arange(rows, dtype=F32))
    col_e = sincos(jnp.arange(GRID_W, dtype=F32))
    pe = jnp.concatenate([jnp.broadcast_to(row_e[:, None, :], (rows, GRID_W, half)),
                          jnp.broadcast_to(col_e[None, :, :], (rows, GRID_W, half))], axis=-1)
    return pe.reshape(SEQ, D)


def _premix_kernel(x_ref, m_ref, g_ref, w_ref, u_ref, v_ref, xb_ref, gb_ref, ga_ref, gr_ref):
    x = x_ref[...]
    h = _rms(x, g_ref[...]) * (1.0 + m_ref[0, 1:2, :]) + m_ref[0, 0:1, :]
    hb = h.astype(BF16)

    def proj(lo, width):
        return jnp.dot(hb, w_ref[:, lo:lo + width], preferred_element_type=F32)

    u_ref[...] = jax.nn.gelu(proj(0, GW)).astype(BF16)
    v_ref[...] = jax.nn.gelu(proj(GW, GW)).astype(BF16)
    xb_ref[...] = proj(2 * GW, LW)
    gb_ref[...] = jax.nn.gelu(proj(2 * GW + LW, LW)).astype(BF16)
    base = 2 * GW + 2 * LW
    for j in range(2):
        ga_ref[:, j * 512:(j + 1) * 512] = jax.nn.sigmoid(proj(base + j * 512, 512)).astype(BF16)
        gr_ref[:, j * 512:(j + 1) * 512] = jax.nn.sigmoid(proj(base + D + j * 512, 512)).astype(BF16)


def _premix(xtok, mods, g_pre, w_in_bf):
    tile = lambda w: pl.BlockSpec((TM, w), lambda i: (i, 0))
    return pl.pallas_call(
        _premix_kernel,
        grid=(N_TILES,),
        in_specs=[tile(D),
                  pl.BlockSpec((1, 8, D), lambda i: (_mod_row(i), 0, 0)),
                  pl.BlockSpec((1, D), lambda i: (0, 0)),
                  pl.BlockSpec((D, IN_WIDTH), lambda i: (0, 0))],
        out_specs=[tile(GW), tile(GW), tile(LW), tile(LW), tile(D), tile(D)],
        out_shape=[jax.ShapeDtypeStruct((NT, GW), BF16), jax.ShapeDtypeStruct((NT, GW), BF16),
                   jax.ShapeDtypeStruct((NT, LW), F32), jax.ShapeDtypeStruct((NT, LW), BF16),
                   jax.ShapeDtypeStruct((NT, D), BF16), jax.ShapeDtypeStruct((NT, D), BF16)],
        compiler_params=_cparams(("parallel",)),
        name="premix",
    )(xtok, mods, g_pre, w_in_bf)


def _lru_block(b, j, reverse):
    k = (LRU_STEPS - 1 - j) if reverse else (j - 1)
    return jnp.where(j == 0, b, BATCH + b * (SEQ // TL) + k), k


def _lru_kernel(x_ref, xp_ref, xn_ref, cw_ref, cb_ref, wg_ref, bg_ref, lam_ref, h_ref,
                xpad, a_buf, d_buf, carry, *, reverse):
    j = pl.program_id(1)
    k = (LRU_STEPS - 1 - j) if reverse else (j - 1)
    has_prev = jnp.logical_and(j > 0, k >= 1)
    has_next = jnp.logical_and(j > 0, k <= SEQ // TL - 2)

    @pl.when(j == 0)
    def _():
        carry[...] = jnp.zeros_like(carry)

    xpad[0:SUBLANES, :] = jnp.where(has_prev, xp_ref[...], 0.0)
    xpad[SUBLANES:SUBLANES + TL, :] = x_ref[...]
    xpad[SUBLANES + TL:, :] = jnp.where(has_next, xn_ref[...], 0.0)
    xc = cb_ref[...]
    for tap in range(CONV_W):
        off = SUBLANES - CONV_LEFT + tap
        xc = xc + xpad[off:off + TL, :] * cw_ref[tap:tap + 1, :]

    gates = jnp.dot(xc.astype(BF16), wg_ref[0], preferred_element_type=F32) + bg_ref[0]
    r = jax.nn.sigmoid(gates[:, :LW])
    ig = jax.nn.sigmoid(gates[:, LW:])
    lam = lam_ref[0]
    softplus_neg = jnp.maximum(-lam, 0.0) + jnp.log1p(jnp.exp(-jnp.abs(lam)))
    log_a = -LRU_C * r * softplus_neg
    a_buf[...] = jnp.exp(log_a)
    th = jnp.tanh(log_a)
    d_buf[...] = jnp.sqrt(-2.0 * th / (1.0 - th)) * (ig * xc)

    row = lax.broadcasted_iota(jnp.int32, (SUBLANES, LW), 0)
    n_grp = TL // SUBLANES

    def body(g, c):
        gi = (n_grp - 1 - g) if reverse else g
        sl = pl.ds(pl.multiple_of(gi * SUBLANES, SUBLANES), SUBLANES)
        a = a_buf[sl, :]
        d = d_buf[sl, :]
        for s in (1, 2, 4):
            shift = (SUBLANES - s) if reverse else s
            m = (row < SUBLANES - s) if reverse else (row >= s)
            d = jnp.where(m, a * pltpu.roll(d, shift, 0) + d, d)
            a = jnp.where(m, a * pltpu.roll(a, shift, 0), a)
        h = d + a * c
        h_ref[sl, :] = h
        edge = 0 if reverse else SUBLANES - 1
        return jnp.broadcast_to(h[edge:edge + 1, :], (SUBLANES, LW))

    carry[...] = lax.fori_loop(0, n_grp, body, carry[...], unroll=4)


def _lru(xb, conv_w, conv_b, wg, bg, lam, reverse):
    n8 = NT // SUBLANES
    per8 = TL // SUBLANES

    def cur(b, j):
        return (_lru_block(b, j, reverse)[0], 0)

    def prev(b, j):
        return (jnp.maximum(_lru_block(b, j, reverse)[0] * per8 - 1, 0), 0)

    def nxt(b, j):
        return (jnp.minimum(_lru_block(b, j, reverse)[0] * per8 + per8, n8 - 1), 0)

    const2 = lambda shape: pl.BlockSpec(shape, lambda b, j: (0, 0))
    const3 = lambda shape: pl.BlockSpec(shape, lambda b, j: (0, 0, 0))
    return pl.pallas_call(
        functools.partial(_lru_kernel, reverse=reverse),
        grid=(BATCH, LRU_STEPS),
        in_specs=[pl.BlockSpec((TL, LW), cur),
                  pl.BlockSpec((SUBLANES, LW), prev),
                  pl.BlockSpec((SUBLANES, LW), nxt),
                  const2((CONV_W, LW)), const2((1, LW)),
                  const3((1, LW, 2 * LW)), const3((1, 1, 2 * LW)), const3((1, 1, LW))],
        out_specs=pl.BlockSpec((TL, LW), cur),
        out_shape=jax.ShapeDtypeStruct((NT, LW), F32),
        scratch_shapes=[pltpu.VMEM((TL + 2 * SUBLANES, LW), F32),
                        pltpu.VMEM((TL, LW), F32), pltpu.VMEM((TL, LW), F32),
                        pltpu.VMEM((SUBLANES, LW), F32)],
        compiler_params=_cparams(("arbitrary", "arbitrary")),
        name="lru_bwd" if reverse else "lru_fwd",
    )(xb, xb, xb, conv_w, conv_b, wg, bg, lam)


def _block_diag(w):
    eye = jnp.eye(HEADS, dtype=w.dtype)
    return jnp.einsum('hij,hg->higj', w, eye).reshape(LW, LW)


def _postmix_kernel(x_ref, m_ref, u_ref, v_ref, gb_ref, ga_ref, gr_ref, hf_ref, hb_ref,
                    lng_ref, lnb_ref, ws_ref, bs_ref, woa_ref, wob_ref, wo_ref, gp_ref, o_ref,
                    mix_buf):
    v = v_ref[...].astype(F32)
    mu = jnp.mean(v, axis=-1, keepdims=True)
    var = jnp.mean(jnp.square(v - mu), axis=-1, keepdims=True)
    vn = ((v - mu) * lax.rsqrt(var + LN_EPS) * lng_ref[...] + lnb_ref[...]).astype(BF16)
    col_grp = lax.broadcasted_iota(jnp.int32, (CHUNK, GW), 1) // (GW // GROUPS)
    for c in range(TM // CHUNK):
        vc = vn[c * CHUNK:(c + 1) * CHUNK, :]
        stacked = jnp.concatenate(
            [jnp.where(col_grp == g, vc, jnp.zeros_like(vc)) for g in range(GROUPS)], axis=0)
        mix_buf[c * CHUNK:(c + 1) * CHUNK, :] = (
            jnp.dot(ws_ref[...], stacked, preferred_element_type=F32) + bs_ref[...])
    ya = (u_ref[...].astype(F32) * mix_buf[...]).astype(BF16)
    za = jnp.dot(ya, woa_ref[...], preferred_element_type=F32)
    yb = ((hf_ref[...] + hb_ref[...]) * gb_ref[...].astype(F32)).astype(BF16)
    zb = jnp.dot(yb, wob_ref[...], preferred_element_type=F32)
    m = (ga_ref[...].astype(F32) * za + gr_ref[...].astype(F32) * zb).astype(BF16)
    y = jnp.dot(m, wo_ref[...], preferred_element_type=F32)
    o_ref[...] = x_ref[...] + m_ref[0, 2:3, :] * _rms(y, gp_ref[...])


def _postmix(xtok, mods, u, v, gb, ga, gr, hf, hb, ln_g, ln_b, ws_cat, bs_full, woa, wob, wo, g_post):
    tile = lambda w: pl.BlockSpec((TM, w), lambda i: (i, 0))
    const = lambda shape: pl.BlockSpec(shape, lambda i: (0, 0))
    return pl.pallas_call(
        _postmix_kernel,
        grid=(N_TILES,),
        in_specs=[tile(D), pl.BlockSpec((1, 8, D), lambda i: (_mod_row(i), 0, 0)),
                  tile(GW), tile(GW), tile(LW), tile(D), tile(D), tile(LW), tile(LW),
                  const((1, GW)), const((1, GW)), const((CHUNK, GROUPS * CHUNK)), const((CHUNK, GW)),
                  const((GW, D)), const((LW, D)), const((D, D)), const((1, D))],
        out_specs=tile(D),
        out_shape=jax.ShapeDtypeStruct((NT, D), F32),
        scratch_shapes=[pltpu.VMEM((TM, GW), F32)],
        compiler_params=_cparams(("parallel",)),
        name="postmix",
    )(xtok, mods, u, v, gb, ga, gr, hf, hb, ln_g, ln_b, ws_cat, bs_full, woa, wob, wo, g_post)


def _route_kernel(x_ref, m_ref, g_ref, rw_ref, rb_ref, tri_ref, low_ref,
                  h_ref, sp_ref, sw_ref, cnt_ref):
    h = _rms(x_ref[...], g_ref[...]) * (1.0 + m_ref[0, 4:5, :]) + m_ref[0, 3:4, :]
    _to_slabs(h_ref, h)

    logits = lax.dot_general(rw_ref[...], h, (((1,), (1,)), ((), ())),
                             precision=lax.Precision.HIGHEST, preferred_element_type=F32)
    s = jax.nn.sigmoid(logits)
    sel = s + rb_ref[...]
    per_grp = N_EXP // N_GRP
    sel3 = sel.reshape(N_GRP, per_grp, TM)
    m1 = jnp.max(sel3, axis=1)
    is_m1 = sel3 == m1[:, None, :]
    n_m1 = jnp.sum(is_m1.astype(F32), axis=1)
    m2 = jnp.max(jnp.where(is_m1, -jnp.inf, sel3), axis=1)
    grp_score = m1 + jnp.where(n_m1 >= 2.0, m1, m2)

    g_iota = lax.broadcasted_iota(jnp.int32, (N_GRP, TM), 0)
    g_rank = jnp.zeros((N_GRP, TM), F32)
    for g in range(N_GRP):
        other = grp_score[g:g + 1, :]
        g_rank = g_rank + jnp.where(other > grp_score, 1.0, 0.0)
        g_rank = g_rank + jnp.where(jnp.logical_and(other == grp_score, g_iota > g), 1.0, 0.0)
    g_keep = jnp.where(g_rank < TOPK_GRP, 1.0, 0.0)
    e_keep = jnp.broadcast_to(g_keep[:, None, :], (N_GRP, per_grp, TM)).reshape(N_EXP, TM)
    selm = jnp.where(e_keep > 0.0, sel, -jnp.inf)

    e_iota = lax.broadcasted_iota(jnp.int32, (N_EXP, TM), 0)
    e_rank = jnp.zeros((N_EXP, TM), F32)
    for e in range(N_EXP):
        other = selm[e:e + 1, :]
        e_rank = e_rank + jnp.where(other > selm, 1.0, 0.0)
        e_rank = e_rank + jnp.where(jnp.logical_and(other == selm, e_iota > e), 1.0, 0.0)
    chosen = e_rank < TOP_K
    sc = jnp.where(chosen, s, 0.0)
    w = sc / jnp.sum(sc, axis=0, keepdims=True) * ROUTE_SCALE

    cum = jnp.dot(jnp.where(chosen, 1.0, 0.0).astype(BF16), tri_ref[...],
                  preferred_element_type=F32)
    tile_cnt = jnp.broadcast_to(cum[:, TM - 1:TM], (N_EXP, LANES))
    cnt_ref[...] = tile_cnt
    tile_off = jnp.dot(low_ref[...], tile_cnt, precision=lax.Precision.HIGHEST,
                       preferred_element_type=F32)[:, 0:1]
    pos = tile_off + cum - 1.0

    for r in range(TOP_K):
        hit = e_rank == float(r)
        sp_ref[r:r + 1, :] = jnp.sum(jnp.where(hit, pos, 0.0), axis=0, keepdims=True).astype(jnp.int32)
        sw_ref[r:r + 1, :] = jnp.sum(jnp.where(hit, w, 0.0), axis=0, keepdims=True)


def _route(xtok, mods, g_pre, rw_t, rb, tri, low):
    const = lambda shape: pl.BlockSpec(shape, lambda i: (0, 0))
    slot = pl.BlockSpec((TOP_K, TM), lambda i: (0, i))
    return pl.pallas_call(
        _route_kernel,
        grid=(N_TILES,),
        in_specs=[pl.BlockSpec((TM, D), lambda i: (i, 0)),
                  pl.BlockSpec((1, 8, D), lambda i: (_mod_row(i), 0, 0)),
                  const((1, D)), const((N_EXP, D)), const((N_EXP, 1)), const((TM, TM)),
                  const((N_EXP, N_EXP))],
        out_specs=[pl.BlockSpec((TM * SLABS, LANES), lambda i: (i, 0)), slot, slot,
                   pl.BlockSpec((N_EXP, LANES), lambda i: (i, 0))],
        out_shape=[jax.ShapeDtypeStruct((NT * SLABS, LANES), F32),
                   jax.ShapeDtypeStruct((TOP_K, NT), jnp.int32),
                   jax.ShapeDtypeStruct((TOP_K, NT), F32),
                   jax.ShapeDtypeStruct((N_TILES * N_EXP, LANES), F32)],
        compiler_params=_cparams(("parallel",)),
        name="route",
    )(xtok, mods, g_pre, rw_t, rb, tri, low)


SEG_BITS = TM.bit_length()


def _start_segments(i, cnt_ref, dst_ref, make_copy):
    def seg(e, off):
        r = cnt_ref[i * N_EXP + e]
        d = dst_ref[i * N_EXP + e]
        for b in range(SEG_BITS):
            n = 1 << b

            @pl.when(((r >> b) & 1) == 1)
            def _():
                o = r & (n - 1)
                make_copy(off + o, d + o, n).start()

        return off + r

    lax.fori_loop(0, N_EXP, seg, 0)


def _wait_segments(make_copy):
    for _ in range(TOP_K):
        make_copy(0, 0, TM).wait()


def _dispatch_kernel(cnt_ref, dst_ref, sp_ref, h_ref, xs_ref, stage, sem):
    i = pl.program_id(0)

    def place(t, carry):
        slab = h_ref[t]
        for k in range(TOP_K):
            stage[sp_ref[k, t]] = slab
        return carry

    lax.fori_loop(0, TM, place, 0, unroll=2)

    def make_copy(src, dst, n):
        return pltpu.make_async_copy(stage.at[pl.ds(src, n)], xs_ref.at[pl.ds(dst, n)], sem)

    _start_segments(i, cnt_ref, dst_ref, make_copy)
    _wait_segments(make_copy)


def _dispatch(seg_cnt, seg_dst, spos, h3):
    grid_spec = pltpu.PrefetchScalarGridSpec(
        num_scalar_prefetch=2,
        grid=(N_TILES,),
        in_specs=[pl.BlockSpec((TOP_K, TM), lambda i, c, d: (0, i), memory_space=pltpu.SMEM),
                  pl.BlockSpec((TM, SLABS, LANES), lambda i, c, d: (i, 0, 0))],
        out_specs=pl.BlockSpec(memory_space=pl.ANY),
        scratch_shapes=[pltpu.VMEM((TOP_K * TM, SLABS, LANES), F32),
                        pltpu.SemaphoreType.DMA],
    )
    return pl.pallas_call(
        _dispatch_kernel,
        grid_spec=grid_spec,
        out_shape=jax.ShapeDtypeStruct((N_ROWS, SLABS, LANES), F32),
        compiler_params=_cparams(("arbitrary",)),
        name="dispatch",
    )(seg_cnt, seg_dst, spos, h3)


def _expert_kernel(ib_ref, ie_ref, lo_ref, hi_ref, ni_ref, x_ref, wgu_ref, wd_ref, y_ref):
    j = pl.program_id(0)

    @pl.when(j < ni_ref[0])
    def _():
        x = _from_slabs(x_ref, MOE_BLOCK).astype(BF16)
        gu = jnp.dot(x, wgu_ref[0], preferred_element_type=F32)
        a = (jax.nn.silu(gu[:, :EDIM]) * gu[:, EDIM:]).astype(BF16)
        y = jnp.dot(a, wd_ref[0], preferred_element_type=F32)
        first = jnp.logical_or(j == 0, ib_ref[j] != ib_ref[jnp.maximum(j - 1, 0)])

        @pl.when(first)
        def _():
            _to_slabs(y_ref, y)

        @pl.when(jnp.logical_not(first))
        def _():
            row = lax.broadcasted_iota(jnp.int32, (MOE_BLOCK, D), 0)
            mine = jnp.logical_and(row >= lo_ref[j], row < hi_ref[j])
            _to_slabs(y_ref, jnp.where(mine, y, _from_slabs(y_ref, MOE_BLOCK)))


def _experts(item_blk, item_exp, item_lo, item_hi, n_items, xs2, wgu, wd):
    def rows(j, ib, ie, lo, hi, ni):
        return (ib[jnp.minimum(j, ni[0] - 1)], 0)

    def wsel(j, ib, ie, lo, hi, ni):
        return (ie[jnp.minimum(j, ni[0] - 1)], 0, 0)

    grid_spec = pltpu.PrefetchScalarGridSpec(
        num_scalar_prefetch=5,
        grid=(N_ITEMS,),
        in_specs=[pl.BlockSpec((MOE_BLOCK * SLABS, LANES), rows),
                  pl.BlockSpec((1, D, 2 * EDIM), wsel),
                  pl.BlockSpec((1, EDIM, D), wsel)],
        out_specs=pl.BlockSpec((MOE_BLOCK * SLABS, LANES), rows),
    )
    return pl.pallas_call(
        _expert_kernel,
        grid_spec=grid_spec,
        out_shape=jax.ShapeDtypeStruct((N_ROWS * SLABS, LANES), F32),
        compiler_params=_cparams(("arbitrary",)),
        name="experts",
    )(item_blk, item_exp, item_lo, item_hi, n_items, xs2, wgu, wd)


def _combine_kernel(cnt_ref, dst_ref, sp_ref, sw_ref, x_ref, m_ref, h_ref, ys_ref, sgu_ref, sd_ref,
                    gp_ref, o_ref, stage, moe_buf, sem):
    i = pl.program_id(0)

    def make_copy(dst, src, n):
        return pltpu.make_async_copy(ys_ref.at[pl.ds(src, n)], stage.at[pl.ds(dst, n)], sem)

    _start_segments(i, cnt_ref, dst_ref, make_copy)

    hb = _from_slabs(h_ref, TM).astype(BF16)
    gu = jnp.dot(hb, sgu_ref[...], preferred_element_type=F32)
    a = (jax.nn.silu(gu[:, :SDIM]) * gu[:, SDIM:]).astype(BF16)
    shared = jnp.dot(a, sd_ref[...], preferred_element_type=F32)

    _wait_segments(make_copy)

    def mix(t, carry):
        acc = sw_ref[0, t] * stage[sp_ref[0, t]]
        for k in range(1, TOP_K):
            acc = acc + sw_ref[k, t] * stage[sp_ref[k, t]]
        moe_buf[pl.ds(pl.multiple_of(t * SLABS, SLABS), SLABS), :] = acc
        return carry

    lax.fori_loop(0, TM, mix, 0, unroll=2)
    f = shared + _from_slabs(moe_buf, TM)
    o_ref[...] = x_ref[...] + m_ref[0, 5:6, :] * _rms(f, gp_ref[...])


def _combine(seg_cnt, seg_dst, spos, sw, xtok, mods, h2, ys3, sgu, sd, g_post):
    const = lambda shape: pl.BlockSpec(shape, lambda i, c, d: (0, 0))
    smem = lambda: pl.BlockSpec((TOP_K, TM), lambda i, c, d: (0, i), memory_space=pltpu.SMEM)
    grid_spec = pltpu.PrefetchScalarGridSpec(
        num_scalar_prefetch=2,
        grid=(N_TILES,),
        in_specs=[smem(), smem(),
                  pl.BlockSpec((TM, D), lambda i, c, d: (i, 0)),
                  pl.BlockSpec((1, 8, D), lambda i, c, d: (_mod_row(i), 0, 0)),
                  pl.BlockSpec((TM * SLABS, LANES), lambda i, c, d: (i, 0)),
                  pl.BlockSpec(memory_space=pl.ANY),
                  const((D, 2 * SDIM)), const((SDIM, D)), const((1, D))],
        out_specs=pl.BlockSpec((TM, D), lambda i, c, d: (i, 0)),
        scratch_shapes=[pltpu.VMEM((TOP_K * TM, SLABS, LANES), F32),
                        pltpu.VMEM((TM * SLABS, LANES), F32),
                        pltpu.SemaphoreType.DMA],
    )
    return pl.pallas_call(
        _combine_kernel,
        grid_spec=grid_spec,
        out_shape=jax.ShapeDtypeStruct((NT, D), F32),
        compiler_params=_cparams(("arbitrary",)),
        name="combine",
    )(seg_cnt, seg_dst, spos, sw, xtok, mods, h2, ys3, sgu, sd, g_post)


def _moe(xtok, mods, g_pre, g_post, router_w, router_bias, e_g, e_u, e_d, s_g, s_u, s_d, tri, low):
    h2, spos, slot_w, cnt = _route(
        xtok, mods, g_pre, router_w.T, router_bias.reshape(N_EXP, 1), tri, low)
    tile_cnt = cnt[:, 0].astype(jnp.int32).reshape(N_TILES, N_EXP)
    before = jnp.cumsum(tile_cnt, axis=0) - tile_cnt
    counts = jnp.sum(tile_cnt, axis=0)
    cend = jnp.cumsum(counts)
    cstart = cend - counts
    seg_cnt = tile_cnt.reshape(-1)
    seg_dst = (cstart[None, :] + before).reshape(-1)
    blk_start = jnp.arange(N_BLOCKS, dtype=jnp.int32) * MOE_BLOCK
    inner = jnp.logical_and(counts > 0, cstart % MOE_BLOCK != 0)
    pos = jnp.sort(jnp.concatenate([blk_start, jnp.where(inner, cstart, N_ROWS)]))
    n_items = jnp.sum((pos < N_ROWS).astype(jnp.int32)).reshape(1)
    pos = jnp.minimum(pos, N_ROWS - 1)
    item_blk = pos // MOE_BLOCK
    item_exp = jnp.sum((cend[None, :] <= pos[:, None]).astype(jnp.int32), axis=1)
    is_exp = item_exp[:, None] == jnp.arange(N_EXP, dtype=jnp.int32)[None, :]
    item_end = jnp.sum(jnp.where(is_exp, cend[None, :], 0), axis=1)
    item_lo = pos - item_blk * MOE_BLOCK
    item_hi = jnp.minimum(item_end - item_blk * MOE_BLOCK, MOE_BLOCK)

    xs3 = _dispatch(seg_cnt, seg_dst, spos, h2.reshape(NT, SLABS, LANES))
    wgu = jnp.concatenate([e_g, e_u], axis=-1).astype(BF16)
    ys2 = _experts(item_blk, item_exp, item_lo, item_hi, n_items,
                   xs3.reshape(N_ROWS * SLABS, LANES), wgu, e_d.astype(BF16))
    sgu = jnp.concatenate([s_g, s_u], axis=-1).astype(BF16)
    return _combine(seg_cnt, seg_dst, spos, slot_w, xtok, mods, h2,
                    ys2.reshape(N_ROWS, SLABS, LANES), sgu, s_d.astype(BF16), g_post)


def kernel(x, c, ctx, c_ctx, w_mod, b_mod, g_pre_mix, g_post_mix, g_pre_ffn, g_post_ffn, w_in, gmlp_ln_g, gmlp_ln_b, gmlp_ws, gmlp_bs, conv_w, conv_b, lru_wa, lru_ba, lru_wx, lru_bx, lru_lambda, w_out_a, w_out_b, w_out, router_w, router_bias, exp_w_gate, exp_w_up, exp_w_down, sh_w_gate, sh_w_up, sh_w_down):
    cond = jnp.concatenate([c_ctx[None, :], c, jnp.zeros((N_COND - 1 - BATCH, D), F32)], axis=0)
    mods_all = _ada(cond, w_mod, b_mod)
    xtok = _entry(ctx.reshape(N_CTX, D), x.reshape(BATCH * SEQ, D), _grid_pos_embed())
    tri = (jnp.arange(TM)[:, None] <= jnp.arange(TM)[None, :]).astype(BF16)
    low = (jnp.arange(N_EXP)[:, None] > jnp.arange(N_EXP)[None, :]).astype(F32)

    for l in range(DEPTH):
        mods = mods_all[l]
        row = lambda p: p[l].reshape(1, -1)
        u, v, xb, gb, ga, gr = _premix(xtok, mods, row(g_pre_mix), w_in[l].astype(BF16))

        hs = []
        for d in range(2):
            wg = jnp.concatenate([_block_diag(lru_wa[l, d]), _block_diag(lru_wx[l, d])], axis=1)
            bg = jnp.concatenate([lru_ba[l, d].reshape(1, LW), lru_bx[l, d].reshape(1, LW)], axis=1)
            hs.append(_lru(xb, conv_w[l], conv_b[l].reshape(1, LW), wg.astype(BF16)[None],
                           bg[None], lru_lambda[l, d].reshape(1, 1, LW), reverse=(d == 1)))

        ws_cat = jnp.transpose(gmlp_ws[l], (1, 0, 2)).reshape(CHUNK, GROUPS * CHUNK).astype(BF16)
        bs_full = jnp.repeat(gmlp_bs[l].T, GW // GROUPS, axis=1)
        xtok = _postmix(xtok, mods, u, v, gb, ga, gr, hs[0], hs[1], row(gmlp_ln_g), row(gmlp_ln_b),
                        ws_cat, bs_full, w_out_a[l].astype(BF16), w_out_b[l].astype(BF16),
                        w_out[l].astype(BF16), row(g_post_mix))

        xtok = _moe(xtok, mods, row(g_pre_ffn), row(g_post_ffn), router_w[l], router_bias[l],
                    exp_w_gate[l], exp_w_up[l], exp_w_down[l],
                    sh_w_gate[l], sh_w_up[l], sh_w_down[l], tri, low)

    return xtok[N_CTX:].reshape(BATCH, SEQ, D)
```

```python
import functools

import jax
import jax.numpy as jnp
from jax import lax
from jax.experimental import pallas as pl
from jax.experimental.pallas import tpu as pltpu

F32 = jnp.float32
BF16 = jnp.bfloat16

D = 1024
BATCH = 4
SEQ = 8192
DEPTH = 2
CTX = 256
GRID_W = 64
POS_BASE = 10000.0
CHUNK = 128
GROUPS = 8
GW = 512
LW = 512
HEADS = 8
HDIM = LW // HEADS
CONV_W = 4
CONV_LEFT = 2
LRU_C = 8.0
IN_WIDTH = 2 * GW + 2 * LW + 2 * D
N_EXP = 64
TOP_K = 8
N_GRP = 8
TOPK_GRP = 4
EDIM = 256
SDIM = 256
ROUTE_SCALE = 2.5
RMS_EPS = 1e-6
LN_EPS = 1e-5

SUBLANES = 8
LANES = 128
SLABS = D // LANES
VMEM_LIMIT = 56 * 1024 * 1024

N_CTX = BATCH * CTX
NT = N_CTX + BATCH * SEQ
TM = 512
N_TILES = NT // TM
CTX_TILES = N_CTX // TM
TILES_PER_SEQ = SEQ // TM
TL = CTX
LRU_STEPS = 1 + SEQ // TL
MOE_BLOCK = 256
N_ROWS = NT * TOP_K
N_BLOCKS = N_ROWS // MOE_BLOCK
N_ITEMS = N_BLOCKS + N_EXP
N_COND = 8


def _cparams(sem):
    return pltpu.CompilerParams(dimension_semantics=sem, vmem_limit_bytes=VMEM_LIMIT)


def _mod_row(i):
    return jnp.where(i < CTX_TILES, 0, 1 + (i - CTX_TILES) // TILES_PER_SEQ)


def _rms(x, g):
    return x * lax.rsqrt(jnp.mean(x * x, axis=-1, keepdims=True) + RMS_EPS) * g


def _to_slabs(ref, val):
    rows = val.shape[0]
    for s in range(SLABS):
        ref[pl.ds(s, rows, stride=SLABS), :] = val[:, s * LANES:(s + 1) * LANES]


def _from_slabs(ref, rows):
    return jnp.concatenate([ref[pl.ds(s, rows, stride=SLABS), :] for s in range(SLABS)], axis=1)


def _ada_kernel(c_ref, w_ref, b_ref, o_ref):
    a = jax.nn.silu(c_ref[...])
    o_ref[0, 0] = jnp.dot(a, w_ref[0], precision=lax.Precision.HIGHEST,
                          preferred_element_type=F32) + b_ref[0, 0]


def _ada(cond, w_mod, b_mod):
    out = pl.pallas_call(
        _ada_kernel,
        grid=(DEPTH, 6),
        in_specs=[pl.BlockSpec((N_COND, D), lambda l, j: (0, 0)),
                  pl.BlockSpec((1, D, D), lambda l, j: (l, 0, j)),
                  pl.BlockSpec((1, 1, 1, D), lambda l, j: (l, j, 0, 0))],
        out_specs=pl.BlockSpec((1, 1, N_COND, D), lambda l, j: (l, j, 0, 0)),
        out_shape=jax.ShapeDtypeStruct((DEPTH, 6, N_COND, D), F32),
        compiler_params=_cparams(("parallel", "parallel")),
        name="ada_params",
    )(cond, w_mod, b_mod.reshape(DEPTH, 6, 1, D))
    mods = jnp.transpose(out, (0, 2, 1, 3))
    return jnp.pad(mods, ((0, 0), (0, 0), (0, 2), (0, 0)))


def _entry_kernel(ctx_ref, x_ref, pe_ref, o_ref):
    i = pl.program_id(0)

    @pl.when(i < CTX_TILES)
    def _():
        o_ref[...] = ctx_ref[...]

    @pl.when(i >= CTX_TILES)
    def _():
        o_ref[...] = x_ref[...] + pe_ref[...]


def _entry(ctx2, x2, pe):
    return pl.pallas_call(
        _entry_kernel,
        grid=(N_TILES,),
        in_specs=[pl.BlockSpec((TM, D), lambda i: (jnp.minimum(i, CTX_TILES - 1), 0)),
                  pl.BlockSpec((TM, D), lambda i: (jnp.maximum(i - CTX_TILES, 0), 0)),
                  pl.BlockSpec((TM, D), lambda i: (jnp.maximum(i - CTX_TILES, 0) % TILES_PER_SEQ, 0))],
        out_specs=pl.BlockSpec((TM, D), lambda i: (i, 0)),
        out_shape=jax.ShapeDtypeStruct((NT, D), F32),
        compiler_params=_cparams(("parallel",)),
        name="entry",
    )(ctx2, x2, pe)


def _grid_pos_embed():
    rows = SEQ // GRID_W
    quarter = D // 4
    half = D // 2
    omega = 1.0 / (POS_BASE ** (jnp.arange(quarter, dtype=F32) / quarter))

    def sincos(p):
        ang = p[:, None] * omega[None, :]
        return jnp.concatenate([jnp.sin(ang), jnp.cos(ang)], axis=-1)

    row_e = sincos(jnp.arange(rows, dtype=F32))
    col_e = sincos(jnp.arange(GRID_W, dtype=F32))
    pe = jnp.concatenate([jnp.broadcast_to(row_e[:, None, :], (rows, GRID_W, half)),
                          jnp.broadcast_to(col_e[None, :, :], (rows, GRID_W, half))], axis=-1)
    return pe.reshape(SEQ, D)


def _premix_kernel(x_ref, m_ref, g_ref, w_ref, u_ref, v_ref, xb_ref, gb_ref, ga_ref, gr_ref):
    x = x_ref[...]
    h = _rms(x, g_ref[...]) * (1.0 + m_ref[0, 1:2, :]) + m_ref[0, 0:1, :]
    hb = h.astype(BF16)

    def proj(lo, width):
        return jnp.dot(hb, w_ref[:, lo:lo + width], preferred_element_type=F32)

    u_ref[...] = jax.nn.gelu(proj(0, GW)).astype(BF16)
    v_ref[...] = jax.nn.gelu(proj(GW, GW)).astype(BF16)
    xb_ref[...] = proj(2 * GW, LW)
    gb_ref[...] = jax.nn.gelu(proj(2 * GW + LW, LW)).astype(BF16)
    base = 2 * GW + 2 * LW
    for j in range(2):
        ga_ref[:, j * 512:(j + 1) * 512] = jax.nn.sigmoid(proj(base + j * 512, 512)).astype(BF16)
        gr_ref[:, j * 512:(j + 1) * 512] = jax.nn.sigmoid(proj(base + D + j * 512, 512)).astype(BF16)


def _premix(xtok, mods, g_pre, w_in_bf):
    tile = lambda w: pl.BlockSpec((TM, w), lambda i: (i, 0))
    return pl.pallas_call(
        _premix_kernel,
        grid=(N_TILES,),
        in_specs=[tile(D),
                  pl.BlockSpec((1, 8, D), lambda i: (_mod_row(i), 0, 0)),
                  pl.BlockSpec((1, D), lambda i: (0, 0)),
                  pl.BlockSpec((D, IN_WIDTH), lambda i: (0, 0))],
        out_specs=[tile(GW), tile(GW), tile(LW), tile(LW), tile(D), tile(D)],
        out_shape=[jax.ShapeDtypeStruct((NT, GW), BF16), jax.ShapeDtypeStruct((NT, GW), BF16),
                   jax.ShapeDtypeStruct((NT, LW), F32), jax.ShapeDtypeStruct((NT, LW), BF16),
                   jax.ShapeDtypeStruct((NT, D), BF16), jax.ShapeDtypeStruct((NT, D), BF16)],
        compiler_params=_cparams(("parallel",)),
        name="premix",
    )(xtok, mods, g_pre, w_in_bf)


def _lru_block(b, j, reverse):
    k = (LRU_STEPS - 1 - j) if reverse else (j - 1)
    return jnp.where(j == 0, b, BATCH + b * (SEQ // TL) + k), k


def _lru_kernel(x_ref, xp_ref, xn_ref, cw_ref, cb_ref, wg_ref, bg_ref, lam_ref, h_ref,
                xpad, a_buf, d_buf, carry, *, reverse):
    j = pl.program_id(1)
    k = (LRU_STEPS - 1 - j) if reverse else (j - 1)
    has_prev = jnp.logical_and(j > 0, k >= 1)
    has_next = jnp.logical_and(j > 0, k <= SEQ // TL - 2)

    @pl.when(j == 0)
    def _():
        carry[...] = jnp.zeros_like(carry)

    xpad[0:SUBLANES, :] = jnp.where(has_prev, xp_ref[...], 0.0)
    xpad[SUBLANES:SUBLANES + TL, :] = x_ref[...]
    xpad[SUBLANES + TL:, :] = jnp.where(has_next, xn_ref[...], 0.0)
    xc = cb_ref[...]
    for tap in range(CONV_W):
        off = SUBLANES - CONV_LEFT + tap
        xc = xc + xpad[off:off + TL, :] * cw_ref[tap:tap + 1, :]

    gates = jnp.dot(xc.astype(BF16), wg_ref[0], preferred_element_type=F32) + bg_ref[0]
    r = jax.nn.sigmoid(gates[:, :LW])
    ig = jax.nn.sigmoid(gates[:, LW:])
    lam = lam_ref[0]
    softplus_neg = jnp.maximum(-lam, 0.0) + jnp.log1p(jnp.exp(-jnp.abs(lam)))
    log_a = -LRU_C * r * softplus_neg
    a_buf[...] = jnp.exp(log_a)
    th = jnp.tanh(log_a)
    d_buf[...] = jnp.sqrt(-2.0 * th / (1.0 - th)) * (ig * xc)

    row = lax.broadcasted_iota(jnp.int32, (SUBLANES, LW), 0)
    n_grp = TL // SUBLANES

    def body(g, c):
        gi = (n_grp - 1 - g) if reverse else g
        sl = pl.ds(pl.multiple_of(gi * SUBLANES, SUBLANES), SUBLANES)
        a = a_buf[sl, :]
        d = d_buf[sl, :]
        for s in (1, 2, 4):
            shift = (SUBLANES - s) if reverse else s
            m = (row < SUBLANES - s) if reverse else (row >= s)
            d = jnp.where(m, a * pltpu.roll(d, shift, 0) + d, d)
            a = jnp.where(m, a * pltpu.roll(a, shift, 0), a)
        h = d + a * c
        h_ref[sl, :] = h
        edge = 0 if reverse else SUBLANES - 1
        return jnp.broadcast_to(h[edge:edge + 1, :], (SUBLANES, LW))

    carry[...] = lax.fori_loop(0, n_grp, body, carry[...], unroll=4)


def _lru(xb, conv_w, conv_b, wg, bg, lam, reverse):
    n8 = NT // SUBLANES
    per8 = TL // SUBLANES

    def cur(b, j):
        return (_lru_block(b, j, reverse)[0], 0)

    def prev(b, j):
        return (jnp.maximum(_lru_block(b, j, reverse)[0] * per8 - 1, 0), 0)

    def nxt(b, j):
        return (jnp.minimum(_lru_block(b, j, reverse)[0] * per8 + per8, n8 - 1), 0)

    const2 = lambda shape: pl.BlockSpec(shape, lambda b, j: (0, 0))
    const3 = lambda shape: pl.BlockSpec(shape, lambda b, j: (0, 0, 0))
    return pl.pallas_call(
        functools.partial(_lru_kernel, reverse=reverse),
        grid=(BATCH, LRU_STEPS),
        in_specs=[pl.BlockSpec((TL, LW), cur),
                  pl.BlockSpec((SUBLANES, LW), prev),
                  pl.BlockSpec((SUBLANES, LW), nxt),
                  const2((CONV_W, LW)), const2((1, LW)),
                  const3((1, LW, 2 * LW)), const3((1, 1, 2 * LW)), const3((1, 1, LW))],
        out_specs=pl.BlockSpec((TL, LW), cur),
        out_shape=jax.ShapeDtypeStruct((NT, LW), F32),
        scratch_shapes=[pltpu.VMEM((TL + 2 * SUBLANES, LW), F32),
                        pltpu.VMEM((TL, LW), F32), pltpu.VMEM((TL, LW), F32),
                        pltpu.VMEM((SUBLANES, LW), F32)],
        compiler_params=_cparams(("arbitrary", "arbitrary")),
        name="lru_bwd" if reverse else "lru_fwd",
    )(xb, xb, xb, conv_w, conv_b, wg, bg, lam)


def _block_diag(w):
    eye = jnp.eye(HEADS, dtype=w.dtype)
    return jnp.einsum('hij,hg->higj', w, eye).reshape(LW, LW)


def _postmix_kernel(x_ref, m_ref, u_ref, v_ref, gb_ref, ga_ref, gr_ref, hf_ref, hb_ref,
                    lng_ref, lnb_ref, ws_ref, bs_ref, woa_ref, wob_ref, wo_ref, gp_ref, o_ref,
                    mix_buf):
    v = v_ref[...].astype(F32)
    mu = jnp.mean(v, axis=-1, keepdims=True)
    var = jnp.mean(jnp.square(v - mu), axis=-1, keepdims=True)
    vn = ((v - mu) * lax.rsqrt(var + LN_EPS) * lng_ref[...] + lnb_ref[...]).astype(BF16)
    col_grp = lax.broadcasted_iota(jnp.int32, (CHUNK, GW), 1) // (GW // GROUPS)
    for c in range(TM // CHUNK):
        vc = vn[c * CHUNK:(c + 1) * CHUNK, :]
        stacked = jnp.concatenate(
            [jnp.where(col_grp == g, vc, jnp.zeros_like(vc)) for g in range(GROUPS)], axis=0)
        mix_buf[c * CHUNK:(c + 1) * CHUNK, :] = (
            jnp.dot(ws_ref[...], stacked, preferred_element_type=F32) + bs_ref[...])
    ya = (u_ref[...].astype(F32) * mix_buf[...]).astype(BF16)
    za = jnp.dot(ya, woa_ref[...], preferred_element_type=F32)
    yb = ((hf_ref[...] + hb_ref[...]) * gb_ref[...].astype(F32)).astype(BF16)
    zb = jnp.dot(yb, wob_ref[...], preferred_element_type=F32)
    m = (ga_ref[...].astype(F32) * za + gr_ref[...].astype(F32) * zb).astype(BF16)
    y = jnp.dot(m, wo_ref[...], preferred_element_type=F32)
    o_ref[...] = x_ref[...] + m_ref[0, 2:3, :] * _rms(y, gp_ref[...])


def _postmix(xtok, mods, u, v, gb, ga, gr, hf, hb, ln_g, ln_b, ws_cat, bs_full, woa, wob, wo, g_post):
    tile = lambda w: pl.BlockSpec((TM, w), lambda i: (i, 0))
    const = lambda shape: pl.BlockSpec(shape, lambda i: (0, 0))
    return pl.pallas_call(
        _postmix_kernel,
        grid=(N_TILES,),
        in_specs=[tile(D), pl.BlockSpec((1, 8, D), lambda i: (_mod_row(i), 0, 0)),
                  tile(GW), tile(GW), tile(LW), tile(D), tile(D), tile(LW), tile(LW),
                  const((1, GW)), const((1, GW)), const((CHUNK, GROUPS * CHUNK)), const((CHUNK, GW)),
                  const((GW, D)), const((LW, D)), const((D, D)), const((1, D))],
        out_specs=tile(D),
        out_shape=jax.ShapeDtypeStruct((NT, D), F32),
        scratch_shapes=[pltpu.VMEM((TM, GW), F32)],
        compiler_params=_cparams(("parallel",)),
        name="postmix",
    )(xtok, mods, u, v, gb, ga, gr, hf, hb, ln_g, ln_b, ws_cat, bs_full, woa, wob, wo, g_post)


def _route_kernel(x_ref, m_ref, g_ref, rw_ref, rb_ref, tri_ref, low_ref,
                  h_ref, sp_ref, sw_ref, cnt_ref):
    h = _rms(x_ref[...], g_ref[...]) * (1.0 + m_ref[0, 4:5, :]) + m_ref[0, 3:4, :]
    _to_slabs(h_ref, h)

    logits = lax.dot_general(rw_ref[...], h, (((1,), (1,)), ((), ())),
                             precision=lax.Precision.HIGHEST, preferred_element_type=F32)
    s = jax.nn.sigmoid(logits)
    sel = s + rb_ref[...]
    per_grp = N_EXP // N_GRP
    sel3 = sel.reshape(N_GRP, per_grp, TM)
    m1 = jnp.max(sel3, axis=1)
    is_m1 = sel3 == m1[:, None, :]
    n_m1 = jnp.sum(is_m1.astype(F32), axis=1)
    m2 = jnp.max(jnp.where(is_m1, -jnp.inf, sel3), axis=1)
    grp_score = m1 + jnp.where(n_m1 >= 2.0, m1, m2)

    g_iota = lax.broadcasted_iota(jnp.int32, (N_GRP, TM), 0)
    g_rank = jnp.zeros((N_GRP, TM), F32)
    for g in range(N_GRP):
        other = grp_score[g:g + 1, :]
        g_rank = g_rank + jnp.where(other > grp_score, 1.0, 0.0)
        g_rank = g_rank + jnp.where(jnp.logical_and(other == grp_score, g_iota > g), 1.0, 0.0)
    g_keep = jnp.where(g_rank < TOPK_GRP, 1.0, 0.0)
    e_keep = jnp.broadcast_to(g_keep[:, None, :], (N_GRP, per_grp, TM)).reshape(N_EXP, TM)
    selm = jnp.where(e_keep > 0.0, sel, -jnp.inf)

    e_iota = lax.broadcasted_iota(jnp.int32, (N_EXP, TM), 0)
    e_rank = jnp.zeros((N_EXP, TM), F32)
    for e in range(N_EXP):
        other = selm[e:e + 1, :]
        e_rank = e_rank + jnp.where(other > selm, 1.0, 0.0)
        e_rank = e_rank + jnp.where(jnp.logical_and(other == selm, e_iota > e), 1.0, 0.0)
    chosen = e_rank < TOP_K
    sc = jnp.where(chosen, s, 0.0)
    w = sc / jnp.sum(sc, axis=0, keepdims=True) * ROUTE_SCALE

    cum = jnp.dot(jnp.where(chosen, 1.0, 0.0).astype(BF16), tri_ref[...],
                  preferred_element_type=F32)
    tile_cnt = jnp.broadcast_to(cum[:, TM - 1:TM], (N_EXP, LANES))
    cnt_ref[...] = tile_cnt
    tile_off = jnp.dot(low_ref[...], tile_cnt, precision=lax.Precision.HIGHEST,
                       preferred_element_type=F32)[:, 0:1]
    pos = tile_off + cum - 1.0

    for r in range(TOP_K):
        hit = e_rank == float(r)
        sp_ref[r:r + 1, :] = jnp.sum(jnp.where(hit, pos, 0.0), axis=0, keepdims=True).astype(jnp.int32)
        sw_ref[r:r + 1, :] = jnp.sum(jnp.where(hit, w, 0.0), axis=0, keepdims=True)


def _route(xtok, mods, g_pre, rw_t, rb, tri, low):
    const = lambda shape: pl.BlockSpec(shape, lambda i: (0, 0))
    slot = pl.BlockSpec((TOP_K, TM), lambda i: (0, i))
    return pl.pallas_call(
        _route_kernel,
        grid=(N_TILES,),
        in_specs=[pl.BlockSpec((TM, D), lambda i: (i, 0)),
                  pl.BlockSpec((1, 8, D), lambda i: (_mod_row(i), 0, 0)),
                  const((1, D)), const((N_EXP, D)), const((N_EXP, 1)), const((TM, TM)),
                  const((N_EXP, N_EXP))],
        out_specs=[pl.BlockSpec((TM * SLABS, LANES), lambda i: (i, 0)), slot, slot,
                   pl.BlockSpec((N_EXP, LANES), lambda i: (i, 0))],
        out_shape=[jax.ShapeDtypeStruct((NT * SLABS, LANES), F32),
                   jax.ShapeDtypeStruct((TOP_K, NT), jnp.int32),
                   jax.ShapeDtypeStruct((TOP_K, NT), F32),
                   jax.ShapeDtypeStruct((N_TILES * N_EXP, LANES), F32)],
        compiler_params=_cparams(("parallel",)),
        name="route",
    )(xtok, mods, g_pre, rw_t, rb, tri, low)


SEG_BITS = TM.bit_length()


def _start_segments(i, cnt_ref, dst_ref, make_copy):
    def seg(e, off):
        r = cnt_ref[i * N_EXP + e]
        d = dst_ref[i * N_EXP + e]
        for b in range(SEG_BITS):
            n = 1 << b

            @pl.when(((r >> b) & 1) == 1)
            def _():
                o = r & (n - 1)
                make_copy(off + o, d + o, n).start()

        return off + r

    lax.fori_loop(0, N_EXP, seg, 0)


def _wait_segments(make_copy):
    for _ in range(TOP_K):
        make_copy(0, 0, TM).wait()


def _per_slot(i, fn):
    for slot in range(2):
        @pl.when(i % 2 == slot)
        def _():
            fn(slot)


STAGE_ROWS = TOP_K * TM


def _dispatch_kernel(cnt_ref, dst_ref, sp_ref, h_ref, xs_ref, stage, sems):
    i = pl.program_id(0)

    def step(slot):
        base = slot * STAGE_ROWS

        def make_copy(src, dst, n):
            return pltpu.make_async_copy(stage.at[pl.ds(base + src, n)], xs_ref.at[pl.ds(dst, n)],
                                         sems.at[slot])

        @pl.when(i >= 2)
        def _():
            _wait_segments(make_copy)

        def place(t, carry):
            slab = h_ref[t]
            for k in range(TOP_K):
                stage[base + sp_ref[t * TOP_K + k]] = slab
            return carry

        lax.fori_loop(0, TM, place, 0, unroll=2)
        _start_segments(i, cnt_ref, dst_ref, make_copy)

        @pl.when(i == N_TILES - 1)
        def _():
            _wait_segments(make_copy)

    _per_slot(i, step)

    @pl.when(i == N_TILES - 1)
    def _():
        other = (N_TILES - 2) % 2

        def other_copy(src, dst, n):
            return pltpu.make_async_copy(stage.at[pl.ds(other * STAGE_ROWS + src, n)],
                                         xs_ref.at[pl.ds(dst, n)], sems.at[other])

        _wait_segments(other_copy)


def _dispatch(seg_cnt, seg_dst, spos_flat, h3):
    grid_spec = pltpu.PrefetchScalarGridSpec(
        num_scalar_prefetch=2,
        grid=(N_TILES,),
        in_specs=[pl.BlockSpec((TM * TOP_K,), lambda i, c, d: (i,), memory_space=pltpu.SMEM),
                  pl.BlockSpec((TM, SLABS, LANES), lambda i, c, d: (i, 0, 0))],
        out_specs=pl.BlockSpec(memory_space=pl.ANY),
        scratch_shapes=[pltpu.VMEM((2 * STAGE_ROWS, SLABS, LANES), F32),
                        pltpu.SemaphoreType.DMA((2,))],
    )
    return pl.pallas_call(
        _dispatch_kernel,
        grid_spec=grid_spec,
        out_shape=jax.ShapeDtypeStruct((N_ROWS, SLABS, LANES), F32),
        compiler_params=_cparams(("arbitrary",)),
        name="dispatch",
    )(seg_cnt, seg_dst, spos_flat, h3)


def _expert_kernel(ib_ref, ie_ref, lo_ref, hi_ref, ni_ref, x_ref, wgu_ref, wd_ref, y_ref):
    j = pl.program_id(0)

    @pl.when(j < ni_ref[0])
    def _():
        x = _from_slabs(x_ref, MOE_BLOCK).astype(BF16)
        gu = jnp.dot(x, wgu_ref[0], preferred_element_type=F32)
        a = (jax.nn.silu(gu[:, :EDIM]) * gu[:, EDIM:]).astype(BF16)
        y = jnp.dot(a, wd_ref[0], preferred_element_type=F32)
        first = jnp.logical_or(j == 0, ib_ref[j] != ib_ref[jnp.maximum(j - 1, 0)])

        @pl.when(first)
        def _():
            _to_slabs(y_ref, y)

        @pl.when(jnp.logical_not(first))
        def _():
            row = lax.broadcasted_iota(jnp.int32, (MOE_BLOCK, D), 0)
            mine = jnp.logical_and(row >= lo_ref[j], row < hi_ref[j])
            _to_slabs(y_ref, jnp.where(mine, y, _from_slabs(y_ref, MOE_BLOCK)))


def _experts(item_blk, item_exp, item_lo, item_hi, n_items, xs2, wgu, wd):
    def rows(j, ib, ie, lo, hi, ni):
        return (ib[jnp.minimum(j, ni[0] - 1)], 0)

    def wsel(j, ib, ie, lo, hi, ni):
        return (ie[jnp.minimum(j, ni[0] - 1)], 0, 0)

    grid_spec = pltpu.PrefetchScalarGridSpec(
        num_scalar_prefetch=5,
        grid=(N_ITEMS,),
        in_specs=[pl.BlockSpec((MOE_BLOCK * SLABS, LANES), rows),
                  pl.BlockSpec((1, D, 2 * EDIM), wsel),
                  pl.BlockSpec((1, EDIM, D), wsel)],
        out_specs=pl.BlockSpec((MOE_BLOCK * SLABS, LANES), rows),
    )
    return pl.pallas_call(
        _expert_kernel,
        grid_spec=grid_spec,
        out_shape=jax.ShapeDtypeStruct((N_ROWS * SLABS, LANES), F32),
        compiler_params=_cparams(("arbitrary",)),
        name="experts",
    )(item_blk, item_exp, item_lo, item_hi, n_items, xs2, wgu, wd)


def _combine_kernel(cnt_ref, dst_ref, sp_ref, sw_ref, x_ref, m_ref, h_ref, ys_ref, sgu_ref, sd_ref,
                    gp_ref, o_ref, stage, moe_buf, sems):
    i = pl.program_id(0)

    def fetch(slot):
        def make_copy(dst, src, n):
            return pltpu.make_async_copy(ys_ref.at[pl.ds(src, n)],
                                         stage.at[pl.ds(slot * STAGE_ROWS + dst, n)], sems.at[slot])
        return make_copy

    @pl.when(i == 0)
    def _():
        _start_segments(0, cnt_ref, dst_ref, fetch(0))

    @pl.when(i + 1 < N_TILES)
    def _():
        _per_slot(i + 1, lambda slot: _start_segments(i + 1, cnt_ref, dst_ref, fetch(slot)))

    hb = _from_slabs(h_ref, TM).astype(BF16)
    gu = jnp.dot(hb, sgu_ref[...], preferred_element_type=F32)
    a = (jax.nn.silu(gu[:, :SDIM]) * gu[:, SDIM:]).astype(BF16)
    shared = jnp.dot(a, sd_ref[...], preferred_element_type=F32)

    def mix_slot(slot):
        base = slot * STAGE_ROWS
        _wait_segments(fetch(slot))

        def mix(t, carry):
            j = t * TOP_K
            acc = sw_ref[j] * stage[base + sp_ref[j]]
            for k in range(1, TOP_K):
                acc = acc + sw_ref[j + k] * stage[base + sp_ref[j + k]]
            moe_buf[pl.ds(pl.multiple_of(t * SLABS, SLABS), SLABS), :] = acc
            return carry

        lax.fori_loop(0, TM, mix, 0, unroll=2)

    _per_slot(i, mix_slot)
    f = shared + _from_slabs(moe_buf, TM)
    o_ref[...] = x_ref[...] + m_ref[0, 5:6, :] * _rms(f, gp_ref[...])


def _combine(seg_cnt, seg_dst, spos, sw, xtok, mods, h2, ys3, sgu, sd, g_post):
    const = lambda shape: pl.BlockSpec(shape, lambda i, c, d: (0, 0))
    smem = lambda: pl.BlockSpec((TM * TOP_K,), lambda i, c, d: (i,), memory_space=pltpu.SMEM)
    grid_spec = pltpu.PrefetchScalarGridSpec(
        num_scalar_prefetch=2,
        grid=(N_TILES,),
        in_specs=[smem(), smem(),
                  pl.BlockSpec((TM, D), lambda i, c, d: (i, 0)),
                  pl.BlockSpec((1, 8, D), lambda i, c, d: (_mod_row(i), 0, 0)),
                  pl.BlockSpec((TM * SLABS, LANES), lambda i, c, d: (i, 0)),
                  pl.BlockSpec(memory_space=pl.ANY),
                  const((D, 2 * SDIM)), const((SDIM, D)), const((1, D))],
        out_specs=pl.BlockSpec((TM, D), lambda i, c, d: (i, 0)),
        scratch_shapes=[pltpu.VMEM((2 * STAGE_ROWS, SLABS, LANES), F32),
                        pltpu.VMEM((TM * SLABS, LANES), F32),
                        pltpu.SemaphoreType.DMA((2,))],
    )
    return pl.pallas_call(
        _combine_kernel,
        grid_spec=grid_spec,
        out_shape=jax.ShapeDtypeStruct((NT, D), F32),
        compiler_params=_cparams(("arbitrary",)),
        name="combine",
    )(seg_cnt, seg_dst, spos, sw, xtok, mods, h2, ys3, sgu, sd, g_post)


def _moe(xtok, mods, g_pre, g_post, router_w, router_bias, e_g, e_u, e_d, s_g, s_u, s_d, tri, low):
    h2, spos, slot_w, cnt = _route(
        xtok, mods, g_pre, router_w.T, router_bias.reshape(N_EXP, 1), tri, low)
    tile_cnt = cnt[:, 0].astype(jnp.int32).reshape(N_TILES, N_EXP)
    before = jnp.cumsum(tile_cnt, axis=0) - tile_cnt
    counts = jnp.sum(tile_cnt, axis=0)
    cend = jnp.cumsum(counts)
    cstart = cend - counts
    seg_cnt = tile_cnt.reshape(-1)
    seg_dst = (cstart[None, :] + before).reshape(-1)
    blk_start = jnp.arange(N_BLOCKS, dtype=jnp.int32) * MOE_BLOCK
    inner = jnp.logical_and(counts > 0, cstart % MOE_BLOCK != 0)
    pos = jnp.sort(jnp.concatenate([blk_start, jnp.where(inner, cstart, N_ROWS)]))
    n_items = jnp.sum((pos < N_ROWS).astype(jnp.int32)).reshape(1)
    pos = jnp.minimum(pos, N_ROWS - 1)
    item_blk = pos // MOE_BLOCK
    item_exp = jnp.sum((cend[None, :] <= pos[:, None]).astype(jnp.int32), axis=1)
    is_exp = item_exp[:, None] == jnp.arange(N_EXP, dtype=jnp.int32)[None, :]
    item_end = jnp.sum(jnp.where(is_exp, cend[None, :], 0), axis=1)
    item_lo = pos - item_blk * MOE_BLOCK
    item_hi = jnp.minimum(item_end - item_blk * MOE_BLOCK, MOE_BLOCK)

    spos = spos.T.reshape(-1)
    slot_w = slot_w.T.reshape(-1)
    xs3 = _dispatch(seg_cnt, seg_dst, spos, h2.reshape(NT, SLABS, LANES))
    wgu = jnp.concatenate([e_g, e_u], axis=-1).astype(BF16)
    ys2 = _experts(item_blk, item_exp, item_lo, item_hi, n_items,
                   xs3.reshape(N_ROWS * SLABS, LANES), wgu, e_d.astype(BF16))
    sgu = jnp.concatenate([s_g, s_u], axis=-1).astype(BF16)
    return _combine(seg_cnt, seg_dst, spos, slot_w, xtok, mods, h2,
                    ys2.reshape(N_ROWS, SLABS, LANES), sgu, s_d.astype(BF16), g_post)


def kernel(x, c, ctx, c_ctx, w_mod, b_mod, g_pre_mix, g_post_mix, g_pre_ffn, g_post_ffn, w_in, gmlp_ln_g, gmlp_ln_b, gmlp_ws, gmlp_bs, conv_w, conv_b, lru_wa, lru_ba, lru_wx, lru_bx, lru_lambda, w_out_a, w_out_b, w_out, router_w, router_bias, exp_w_gate, exp_w_up, exp_w_down, sh_w_gate, sh_w_up, sh_w_down):
    cond = jnp.concatenate([c_ctx[None, :], c, jnp.zeros((N_COND - 1 - BATCH, D), F32)], axis=0)
    mods_all = _ada(cond, w_mod, b_mod)
    xtok = _entry(ctx.reshape(N_CTX, D), x.reshape(BATCH * SEQ, D), _grid_pos_embed())
    tri = (jnp.arange(TM)[:, None] <= jnp.arange(TM)[None, :]).astype(BF16)
    low = (jnp.arange(N_EXP)[:, None] > jnp.arange(N_EXP)[None, :]).astype(F32)

    for l in range(DEPTH):
        mods = mods_all[l]
        row = lambda p: p[l].reshape(1, -1)
        u, v, xb, gb, ga, gr = _premix(xtok, mods, row(g_pre_mix), w_in[l].astype(BF16))

        hs = []
        for d in range(2):
            wg = jnp.concatenate([_block_diag(lru_wa[l, d]), _block_diag(lru_wx[l, d])], axis=1)
            bg = jnp.concatenate([lru_ba[l, d].reshape(1, LW), lru_bx[l, d].reshape(1, LW)], axis=1)
            hs.append(_lru(xb, conv_w[l], conv_b[l].reshape(1, LW), wg.astype(BF16)[None],
                           bg[None], lru_lambda[l, d].reshape(1, 1, LW), reverse=(d == 1)))

        ws_cat = jnp.transpose(gmlp_ws[l], (1, 0, 2)).reshape(CHUNK, GROUPS * CHUNK).astype(BF16)
        bs_full = jnp.repeat(gmlp_bs[l].T, GW // GROUPS, axis=1)
        xtok = _postmix(xtok, mods, u, v, gb, ga, gr, hs[0], hs[1], row(gmlp_ln_g), row(gmlp_ln_b),
                        ws_cat, bs_full, w_out_a[l].astype(BF16), w_out_b[l].astype(BF16),
                        w_out[l].astype(BF16), row(g_post_mix))

        xtok = _moe(xtok, mods, row(g_pre_ffn), row(g_post_ffn), router_w[l], router_bias[l],
                    exp_w_gate[l], exp_w_up[l], exp_w_down[l],
                    sh_w_gate[l], sh_w_up[l], sh_w_down[l], tri, low)

    return xtok[N_CTX:].reshape(BATCH, SEQ, D)
```

```python
import functools

import jax
import jax.numpy as jnp
from jax import lax
from jax.experimental import pallas as pl
from jax.experimental.pallas import tpu as pltpu

F32 = jnp.float32
BF16 = jnp.bfloat16

D = 1024
BATCH = 4
SEQ = 8192
DEPTH = 2
CTX = 256
GRID_W = 64
POS_BASE = 10000.0
CHUNK = 128
GROUPS = 8
GW = 512
LW = 512
HEADS = 8
HDIM = LW // HEADS
CONV_W = 4
CONV_LEFT = 2
LRU_C = 8.0
IN_WIDTH = 2 * GW + 2 * LW + 2 * D
N_EXP = 64
TOP_K = 8
N_GRP = 8
TOPK_GRP = 4
EDIM = 256
SDIM = 256
ROUTE_SCALE = 2.5
RMS_EPS = 1e-6
LN_EPS = 1e-5

SUBLANES = 8
LANES = 128
SLABS = D // LANES
VMEM_LIMIT = 56 * 1024 * 1024

N_CTX = BATCH * CTX
NT = N_CTX + BATCH * SEQ
TM = 512
N_TILES = NT // TM
CTX_TILES = N_CTX // TM
TILES_PER_SEQ = SEQ // TM
TL = CTX
LRU_STEPS = 1 + SEQ // TL
MOE_BLOCK = 512
MOE_SUB = 512
N_ROWS = NT * TOP_K
N_BLOCKS = N_ROWS // MOE_BLOCK
N_ITEMS = N_BLOCKS + N_EXP
N_COND = 8


def _cparams(sem):
    return pltpu.CompilerParams(dimension_semantics=sem, vmem_limit_bytes=VMEM_LIMIT)


def _mod_row(i):
    return jnp.where(i < CTX_TILES, 0, 1 + (i - CTX_TILES) // TILES_PER_SEQ)


def _rms(x, g):
    return x * lax.rsqrt(jnp.mean(x * x, axis=-1, keepdims=True) + RMS_EPS) * g


def _to_slabs(ref, val, row0=0):
    rows = val.shape[0]
    for s in range(SLABS):
        ref[pl.ds(row0 * SLABS + s, rows, stride=SLABS), :] = val[:, s * LANES:(s + 1) * LANES]


def _from_slabs(ref, rows, row0=0):
    return jnp.concatenate(
        [ref[pl.ds(row0 * SLABS + s, rows, stride=SLABS), :] for s in range(SLABS)], axis=1)


def _ada_kernel(c_ref, w_ref, b_ref, o_ref):
    a = jax.nn.silu(c_ref[...])
    o_ref[0, 0] = jnp.dot(a, w_ref[0], precision=lax.Precision.HIGHEST,
                          preferred_element_type=F32) + b_ref[0, 0]


def _ada(cond, w_mod, b_mod):
    out = pl.pallas_call(
        _ada_kernel,
        grid=(DEPTH, 6),
        in_specs=[pl.BlockSpec((N_COND, D), lambda l, j: (0, 0)),
                  pl.BlockSpec((1, D, D), lambda l, j: (l, 0, j)),
                  pl.BlockSpec((1, 1, 1, D), lambda l, j: (l, j, 0, 0))],
        out_specs=pl.BlockSpec((1, 1, N_COND, D), lambda l, j: (l, j, 0, 0)),
        out_shape=jax.ShapeDtypeStruct((DEPTH, 6, N_COND, D), F32),
        compiler_params=_cparams(("parallel", "parallel")),
        name="ada_params",
    )(cond, w_mod, b_mod.reshape(DEPTH, 6, 1, D))
    mods = jnp.transpose(out, (0, 2, 1, 3))
    return jnp.pad(mods, ((0, 0), (0, 0), (0, 2), (0, 0)))


def _entry_kernel(ctx_ref, x_ref, pe_ref, o_ref):
    i = pl.program_id(0)

    @pl.when(i < CTX_TILES)
    def _():
        o_ref[...] = ctx_ref[...]

    @pl.when(i >= CTX_TILES)
    def _():
        o_ref[...] = x_ref[...] + pe_ref[...]


def _entry(ctx2, x2, pe):
    return pl.pallas_call(
        _entry_kernel,
        grid=(N_TILES,),
        in_specs=[pl.BlockSpec((TM, D), lambda i: (jnp.minimum(i, CTX_TILES - 1), 0)),
                  pl.BlockSpec((TM, D), lambda i: (jnp.maximum(i - CTX_TILES, 0), 0)),
                  pl.BlockSpec((TM, D), lambda i: (jnp.maximum(i - CTX_TILES, 0) % TILES_PER_SEQ, 0))],
        out_specs=pl.BlockSpec((TM, D), lambda i: (i, 0)),
        out_shape=jax.ShapeDtypeStruct((NT, D), F32),
        compiler_params=_cparams(("parallel",)),
        name="entry",
    )(ctx2, x2, pe)


def _grid_pos_embed():
    rows = SEQ // GRID_W
    quarter = D // 4
    half = D // 2
    omega = 1.0 / (POS_BASE ** (jnp.arange(quarter, dtype=F32) / quarter))

    def sincos(p):
        ang = p[:, None] * omega[None, :]
        return jnp.concatenate([jnp.sin(ang), jnp.cos(ang)], axis=-1)

    row_e = sincos(jnp.arange(rows, dtype=F32))
    col_e = sincos(jnp.arange(GRID_W, dtype=F32))
    pe = jnp.concatenate([jnp.broadcast_to(row_e[:, None, :], (rows, GRID_W, half)),
                          jnp.broadcast_to(col_e[None, :, :], (rows, GRID_W, half))], axis=-1)
    return pe.reshape(SEQ, D)


def _premix_kernel(x_ref, m_ref, g_ref, w_ref, u_ref, v_ref, xb_ref, gb_ref, ga_ref, gr_ref):
    x = x_ref[...]
    h = _rms(x, g_ref[...]) * (1.0 + m_ref[0, 1:2, :]) + m_ref[0, 0:1, :]
    hb = h.astype(BF16)

    def proj(lo, width):
        return jnp.dot(hb, w_ref[:, lo:lo + width], preferred_element_type=F32)

    u_ref[...] = jax.nn.gelu(proj(0, GW)).astype(BF16)
    v_ref[...] = jax.nn.gelu(proj(GW, GW)).astype(BF16)
    xb_ref[...] = proj(2 * GW, LW)
    gb_ref[...] = jax.nn.gelu(proj(2 * GW + LW, LW)).astype(BF16)
    base = 2 * GW + 2 * LW
    for j in range(2):
        ga_ref[:, j * 512:(j + 1) * 512] = jax.nn.sigmoid(proj(base + j * 512, 512)).astype(BF16)
        gr_ref[:, j * 512:(j + 1) * 512] = jax.nn.sigmoid(proj(base + D + j * 512, 512)).astype(BF16)


def _premix(xtok, mods, g_pre, w_in_bf):
    tile = lambda w: pl.BlockSpec((TM, w), lambda i: (i, 0))
    return pl.pallas_call(
        _premix_kernel,
        grid=(N_TILES,),
        in_specs=[tile(D),
                  pl.BlockSpec((1, 8, D), lambda i: (_mod_row(i), 0, 0)),
                  pl.BlockSpec((1, D), lambda i: (0, 0)),
                  pl.BlockSpec((D, IN_WIDTH), lambda i: (0, 0))],
        out_specs=[tile(GW), tile(GW), tile(LW), tile(LW), tile(D), tile(D)],
        out_shape=[jax.ShapeDtypeStruct((NT, GW), BF16), jax.ShapeDtypeStruct((NT, GW), BF16),
                   jax.ShapeDtypeStruct((NT, LW), F32), jax.ShapeDtypeStruct((NT, LW), BF16),
                   jax.ShapeDtypeStruct((NT, D), BF16), jax.ShapeDtypeStruct((NT, D), BF16)],
        compiler_params=_cparams(("parallel",)),
        name="premix",
    )(xtok, mods, g_pre, w_in_bf)


def _lru_block(b, j, reverse):
    k = (LRU_STEPS - 1 - j) if reverse else (j - 1)
    return jnp.where(j == 0, b, BATCH + b * (SEQ // TL) + k), k


def _lru_kernel(x_ref, xp_ref, xn_ref, cw_ref, cb_ref, wg_ref, bg_ref, lam_ref, h_ref,
                xpad, a_buf, d_buf, carry, *, reverse):
    j = pl.program_id(1)
    k = (LRU_STEPS - 1 - j) if reverse else (j - 1)
    has_prev = jnp.logical_and(j > 0, k >= 1)
    has_next = jnp.logical_and(j > 0, k <= SEQ // TL - 2)

    @pl.when(j == 0)
    def _():
        carry[...] = jnp.zeros_like(carry)

    xpad[0:SUBLANES, :] = jnp.where(has_prev, xp_ref[...], 0.0)
    xpad[SUBLANES:SUBLANES + TL, :] = x_ref[...]
    xpad[SUBLANES + TL:, :] = jnp.where(has_next, xn_ref[...], 0.0)
    xc = cb_ref[...]
    for tap in range(CONV_W):
        off = SUBLANES - CONV_LEFT + tap
        xc = xc + xpad[off:off + TL, :] * cw_ref[tap:tap + 1, :]

    gates = jnp.dot(xc.astype(BF16), wg_ref[0], preferred_element_type=F32) + bg_ref[0]
    r = jax.nn.sigmoid(gates[:, :LW])
    ig = jax.nn.sigmoid(gates[:, LW:])
    lam = lam_ref[0]
    softplus_neg = jnp.maximum(-lam, 0.0) + jnp.log1p(jnp.exp(-jnp.abs(lam)))
    log_a = -LRU_C * r * softplus_neg
    a_buf[...] = jnp.exp(log_a)
    th = jnp.tanh(log_a)
    d_buf[...] = jnp.sqrt(-2.0 * th / (1.0 - th)) * (ig * xc)

    row = lax.broadcasted_iota(jnp.int32, (SUBLANES, LW), 0)
    n_grp = TL // SUBLANES

    def body(g, c):
        gi = (n_grp - 1 - g) if reverse else g
        sl = pl.ds(pl.multiple_of(gi * SUBLANES, SUBLANES), SUBLANES)
        a = a_buf[sl, :]
        d = d_buf[sl, :]
        for s in (1, 2, 4):
            shift = (SUBLANES - s) if reverse else s
            m = (row < SUBLANES - s) if reverse else (row >= s)
            d = jnp.where(m, a * pltpu.roll(d, shift, 0) + d, d)
            a = jnp.where(m, a * pltpu.roll(a, shift, 0), a)
        h = d + a * c
        h_ref[sl, :] = h
        edge = 0 if reverse else SUBLANES - 1
        return jnp.broadcast_to(h[edge:edge + 1, :], (SUBLANES, LW))

    carry[...] = lax.fori_loop(0, n_grp, body, carry[...], unroll=4)


def _lru(xb, conv_w, conv_b, wg, bg, lam, reverse):
    n8 = NT // SUBLANES
    per8 = TL // SUBLANES

    def cur(b, j):
        return (_lru_block(b, j, reverse)[0], 0)

    def prev(b, j):
        return (jnp.maximum(_lru_block(b, j, reverse)[0] * per8 - 1, 0), 0)

    def nxt(b, j):
        return (jnp.minimum(_lru_block(b, j, reverse)[0] * per8 + per8, n8 - 1), 0)

    const2 = lambda shape: pl.BlockSpec(shape, lambda b, j: (0, 0))
    const3 = lambda shape: pl.BlockSpec(shape, lambda b, j: (0, 0, 0))
    return pl.pallas_call(
        functools.partial(_lru_kernel, reverse=reverse),
        grid=(BATCH, LRU_STEPS),
        in_specs=[pl.BlockSpec((TL, LW), cur),
                  pl.BlockSpec((SUBLANES, LW), prev),
                  pl.BlockSpec((SUBLANES, LW), nxt),
                  const2((CONV_W, LW)), const2((1, LW)),
                  const3((1, LW, 2 * LW)), const3((1, 1, 2 * LW)), const3((1, 1, LW))],
        out_specs=pl.BlockSpec((TL, LW), cur),
        out_shape=jax.ShapeDtypeStruct((NT, LW), F32),
        scratch_shapes=[pltpu.VMEM((TL + 2 * SUBLANES, LW), F32),
                        pltpu.VMEM((TL, LW), F32), pltpu.VMEM((TL, LW), F32),
                        pltpu.VMEM((SUBLANES, LW), F32)],
        compiler_params=_cparams(("arbitrary", "arbitrary")),
        name="lru_bwd" if reverse else "lru_fwd",
    )(xb, xb, xb, conv_w, conv_b, wg, bg, lam)


def _block_diag(w):
    eye = jnp.eye(HEADS, dtype=w.dtype)
    return jnp.einsum('hij,hg->higj', w, eye).reshape(LW, LW)


def _postmix_kernel(x_ref, m_ref, u_ref, v_ref, gb_ref, ga_ref, gr_ref, hf_ref, hb_ref,
                    lng_ref, lnb_ref, ws_ref, bs_ref, woa_ref, wob_ref, wo_ref, gp_ref, o_ref,
                    mix_buf):
    v = v_ref[...].astype(F32)
    mu = jnp.mean(v, axis=-1, keepdims=True)
    var = jnp.mean(jnp.square(v - mu), axis=-1, keepdims=True)
    vn = ((v - mu) * lax.rsqrt(var + LN_EPS) * lng_ref[...] + lnb_ref[...]).astype(BF16)
    col_grp = lax.broadcasted_iota(jnp.int32, (CHUNK, GW), 1) // (GW // GROUPS)
    for c in range(TM // CHUNK):
        vc = vn[c * CHUNK:(c + 1) * CHUNK, :]
        stacked = jnp.concatenate(
            [jnp.where(col_grp == g, vc, jnp.zeros_like(vc)) for g in range(GROUPS)], axis=0)
        mix_buf[c * CHUNK:(c + 1) * CHUNK, :] = (
            jnp.dot(ws_ref[...], stacked, preferred_element_type=F32) + bs_ref[...])
    ya = (u_ref[...].astype(F32) * mix_buf[...]).astype(BF16)
    za = jnp.dot(ya, woa_ref[...], preferred_element_type=F32)
    yb = ((hf_ref[...] + hb_ref[...]) * gb_ref[...].astype(F32)).astype(BF16)
    zb = jnp.dot(yb, wob_ref[...], preferred_element_type=F32)
    m = (ga_ref[...].astype(F32) * za + gr_ref[...].astype(F32) * zb).astype(BF16)
    y = jnp.dot(m, wo_ref[...], preferred_element_type=F32)
    o_ref[...] = x_ref[...] + m_ref[0, 2:3, :] * _rms(y, gp_ref[...])


def _postmix(xtok, mods, u, v, gb, ga, gr, hf, hb, ln_g, ln_b, ws_cat, bs_full, woa, wob, wo, g_post):
    tile = lambda w: pl.BlockSpec((TM, w), lambda i: (i, 0))
    const = lambda shape: pl.BlockSpec(shape, lambda i: (0, 0))
    return pl.pallas_call(
        _postmix_kernel,
        grid=(N_TILES,),
        in_specs=[tile(D), pl.BlockSpec((1, 8, D), lambda i: (_mod_row(i), 0, 0)),
                  tile(GW), tile(GW), tile(LW), tile(D), tile(D), tile(LW), tile(LW),
                  const((1, GW)), const((1, GW)), const((CHUNK, GROUPS * CHUNK)), const((CHUNK, GW)),
                  const((GW, D)), const((LW, D)), const((D, D)), const((1, D))],
        out_specs=tile(D),
        out_shape=jax.ShapeDtypeStruct((NT, D), F32),
        scratch_shapes=[pltpu.VMEM((TM, GW), F32)],
        compiler_params=_cparams(("parallel",)),
        name="postmix",
    )(xtok, mods, u, v, gb, ga, gr, hf, hb, ln_g, ln_b, ws_cat, bs_full, woa, wob, wo, g_post)


def _route_kernel(x_ref, m_ref, g_ref, rw_ref, rb_ref, tri_ref, low_ref,
                  h_ref, sp_ref, sw_ref, cnt_ref):
    h = _rms(x_ref[...], g_ref[...]) * (1.0 + m_ref[0, 4:5, :]) + m_ref[0, 3:4, :]
    _to_slabs(h_ref, h)

    logits = lax.dot_general(rw_ref[...], h, (((1,), (1,)), ((), ())),
                             precision=lax.Precision.HIGHEST, preferred_element_type=F32)
    s = jax.nn.sigmoid(logits)
    sel = s + rb_ref[...]
    per_grp = N_EXP // N_GRP
    sel3 = sel.reshape(N_GRP, per_grp, TM)
    m1 = jnp.max(sel3, axis=1)
    is_m1 = sel3 == m1[:, None, :]
    n_m1 = jnp.sum(is_m1.astype(F32), axis=1)
    m2 = jnp.max(jnp.where(is_m1, -jnp.inf, sel3), axis=1)
    grp_score = m1 + jnp.where(n_m1 >= 2.0, m1, m2)

    g_iota = lax.broadcasted_iota(jnp.int32, (N_GRP, TM), 0)
    g_rank = jnp.zeros((N_GRP, TM), F32)
    for g in range(N_GRP):
        other = grp_score[g:g + 1, :]
        g_rank = g_rank + jnp.where(other > grp_score, 1.0, 0.0)
        g_rank = g_rank + jnp.where(jnp.logical_and(other == grp_score, g_iota > g), 1.0, 0.0)
    g_keep = jnp.where(g_rank < TOPK_GRP, 1.0, 0.0)
    e_keep = jnp.broadcast_to(g_keep[:, None, :], (N_GRP, per_grp, TM)).reshape(N_EXP, TM)
    selm = jnp.where(e_keep > 0.0, sel, -jnp.inf)

    e_iota = lax.broadcasted_iota(jnp.int32, (N_EXP, TM), 0)
    e_rank = jnp.zeros((N_EXP, TM), F32)
    for e in range(N_EXP):
        other = selm[e:e + 1, :]
        e_rank = e_rank + jnp.where(other > selm, 1.0, 0.0)
        e_rank = e_rank + jnp.where(jnp.logical_and(other == selm, e_iota > e), 1.0, 0.0)
    chosen = e_rank < TOP_K
    sc = jnp.where(chosen, s, 0.0)
    w = sc / jnp.sum(sc, axis=0, keepdims=True) * ROUTE_SCALE

    cum = jnp.dot(jnp.where(chosen, 1.0, 0.0).astype(BF16), tri_ref[...],
                  preferred_element_type=F32)
    tile_cnt = jnp.broadcast_to(cum[:, TM - 1:TM], (N_EXP, LANES))
    cnt_ref[...] = tile_cnt
    tile_off = jnp.dot(low_ref[...], tile_cnt, precision=lax.Precision.HIGHEST,
                       preferred_element_type=F32)[:, 0:1]
    pos = tile_off + cum - 1.0

    for r in range(TOP_K):
        hit = e_rank == float(r)
        sp_ref[r:r + 1, :] = jnp.sum(jnp.where(hit, pos, 0.0), axis=0, keepdims=True).astype(jnp.int32)
        sw_ref[r:r + 1, :] = jnp.sum(jnp.where(hit, w, 0.0), axis=0, keepdims=True)


def _route(xtok, mods, g_pre, rw_t, rb, tri, low):
    const = lambda shape: pl.BlockSpec(shape, lambda i: (0, 0))
    slot = pl.BlockSpec((TOP_K, TM), lambda i: (0, i))
    return pl.pallas_call(
        _route_kernel,
        grid=(N_TILES,),
        in_specs=[pl.BlockSpec((TM, D), lambda i: (i, 0)),
                  pl.BlockSpec((1, 8, D), lambda i: (_mod_row(i), 0, 0)),
                  const((1, D)), const((N_EXP, D)), const((N_EXP, 1)), const((TM, TM)),
                  const((N_EXP, N_EXP))],
        out_specs=[pl.BlockSpec((TM * SLABS, LANES), lambda i: (i, 0)), slot, slot,
                   pl.BlockSpec((N_EXP, LANES), lambda i: (i, 0))],
        out_shape=[jax.ShapeDtypeStruct((NT * SLABS, LANES), F32),
                   jax.ShapeDtypeStruct((TOP_K, NT), jnp.int32),
                   jax.ShapeDtypeStruct((TOP_K, NT), F32),
                   jax.ShapeDtypeStruct((N_TILES * N_EXP, LANES), F32)],
        compiler_params=_cparams(("parallel",)),
        name="route",
    )(xtok, mods, g_pre, rw_t, rb, tri, low)


SEG_BITS = TM.bit_length()


def _start_segments(i, cnt_ref, dst_ref, make_copy):
    def seg(e, off):
        r = cnt_ref[i * N_EXP + e]
        d = dst_ref[i * N_EXP + e]
        for b in range(SEG_BITS):
            n = 1 << b

            @pl.when(((r >> b) & 1) == 1)
            def _():
                o = r & (n - 1)
                make_copy(off + o, d + o, n).start()

        return off + r

    lax.fori_loop(0, N_EXP, seg, 0)


def _wait_segments(make_copy):
    for _ in range(TOP_K):
        make_copy(0, 0, TM).wait()


def _per_slot(i, fn):
    for slot in range(2):
        @pl.when(i % 2 == slot)
        def _():
            fn(slot)


STAGE_ROWS = TOP_K * TM


def _dispatch_kernel(cnt_ref, dst_ref, sp_ref, h_ref, xs_ref, stage, sems):
    i = pl.program_id(0)

    def step(slot):
        base = slot * STAGE_ROWS

        def make_copy(src, dst, n):
            return pltpu.make_async_copy(stage.at[pl.ds(base + src, n)], xs_ref.at[pl.ds(dst, n)],
                                         sems.at[slot])

        @pl.when(i >= 2)
        def _():
            _wait_segments(make_copy)

        slot_rows = stage.at[pl.ds(base, STAGE_ROWS)]

        def place(t, carry):
            slab = h_ref[t]
            for k in range(TOP_K):
                slot_rows[sp_ref[t * TOP_K + k]] = slab
            return carry

        lax.fori_loop(0, TM, place, 0, unroll=2)
        _start_segments(i, cnt_ref, dst_ref, make_copy)

        @pl.when(i == N_TILES - 1)
        def _():
            _wait_segments(make_copy)

    _per_slot(i, step)

    @pl.when(i == N_TILES - 1)
    def _():
        other = (N_TILES - 2) % 2

        def other_copy(src, dst, n):
            return pltpu.make_async_copy(stage.at[pl.ds(other * STAGE_ROWS + src, n)],
                                         xs_ref.at[pl.ds(dst, n)], sems.at[other])

        _wait_segments(other_copy)


def _dispatch(seg_cnt, seg_dst, spos_flat, h3):
    grid_spec = pltpu.PrefetchScalarGridSpec(
        num_scalar_prefetch=2,
        grid=(N_TILES,),
        in_specs=[pl.BlockSpec((TM * TOP_K,), lambda i, c, d: (i,), memory_space=pltpu.SMEM),
                  pl.BlockSpec((TM, SLABS, LANES), lambda i, c, d: (i, 0, 0))],
        out_specs=pl.BlockSpec(memory_space=pl.ANY),
        scratch_shapes=[pltpu.VMEM((2 * STAGE_ROWS, SLABS, LANES), F32),
                        pltpu.SemaphoreType.DMA((2,))],
    )
    return pl.pallas_call(
        _dispatch_kernel,
        grid_spec=grid_spec,
        out_shape=jax.ShapeDtypeStruct((N_ROWS, SLABS, LANES), F32),
        compiler_params=_cparams(("arbitrary",)),
        name="dispatch",
    )(seg_cnt, seg_dst, spos_flat, h3)


def _expert_kernel(ib_ref, ie_ref, lo_ref, hi_ref, ni_ref, x_ref, wgu_ref, wd_ref, y_ref):
    j = pl.program_id(0)

    def swiglu(r0):
        x = _from_slabs(x_ref, MOE_SUB, r0).astype(BF16)
        gu = jnp.dot(x, wgu_ref[0], preferred_element_type=F32)
        a = (jax.nn.silu(gu[:, :EDIM]) * gu[:, EDIM:]).astype(BF16)
        return jnp.dot(a, wd_ref[0], preferred_element_type=F32)

    @pl.when(j < ni_ref[0])
    def _():
        first = jnp.logical_or(j == 0, ib_ref[j] != ib_ref[jnp.maximum(j - 1, 0)])

        @pl.when(first)
        def _():
            for r0 in range(0, MOE_BLOCK, MOE_SUB):
                _to_slabs(y_ref, swiglu(r0), r0)

        @pl.when(jnp.logical_not(first))
        def _():
            for r0 in range(0, MOE_BLOCK, MOE_SUB):
                row = r0 + lax.broadcasted_iota(jnp.int32, (MOE_SUB, D), 0)
                mine = jnp.logical_and(row >= lo_ref[j], row < hi_ref[j])
                _to_slabs(y_ref, jnp.where(mine, swiglu(r0), _from_slabs(y_ref, MOE_SUB, r0)), r0)


def _experts(item_blk, item_exp, item_lo, item_hi, n_items, xs2, wgu, wd):
    def rows(j, ib, ie, lo, hi, ni):
        return (ib[jnp.minimum(j, ni[0] - 1)], 0)

    def wsel(j, ib, ie, lo, hi, ni):
        return (ie[jnp.minimum(j, ni[0] - 1)], 0, 0)

    grid_spec = pltpu.PrefetchScalarGridSpec(
        num_scalar_prefetch=5,
        grid=(N_ITEMS,),
        in_specs=[pl.BlockSpec((MOE_BLOCK * SLABS, LANES), rows),
                  pl.BlockSpec((1, D, 2 * EDIM), wsel),
                  pl.BlockSpec((1, EDIM, D), wsel)],
        out_specs=pl.BlockSpec((MOE_BLOCK * SLABS, LANES), rows),
    )
    return pl.pallas_call(
        _expert_kernel,
        grid_spec=grid_spec,
        out_shape=jax.ShapeDtypeStruct((N_ROWS * SLABS, LANES), F32),
        compiler_params=_cparams(("arbitrary",)),
        name="experts",
    )(item_blk, item_exp, item_lo, item_hi, n_items, xs2, wgu, wd)


def _combine_kernel(cnt_ref, dst_ref, sp_ref, sw_ref, x_ref, m_ref, h_ref, ys_ref, sgu_ref, sd_ref,
                    gp_ref, o_ref, stage, moe_buf, sems):
    i = pl.program_id(0)

    def fetch(slot):
        def make_copy(dst, src, n):
            return pltpu.make_async_copy(ys_ref.at[pl.ds(src, n)],
                                         stage.at[pl.ds(slot * STAGE_ROWS + dst, n)], sems.at[slot])
        return make_copy

    @pl.when(i == 0)
    def _():
        _start_segments(0, cnt_ref, dst_ref, fetch(0))

    @pl.when(i + 1 < N_TILES)
    def _():
        _per_slot(i + 1, lambda slot: _start_segments(i + 1, cnt_ref, dst_ref, fetch(slot)))

    hb = _from_slabs(h_ref, TM).astype(BF16)
    gu = jnp.dot(hb, sgu_ref[...], preferred_element_type=F32)
    a = (jax.nn.silu(gu[:, :SDIM]) * gu[:, SDIM:]).astype(BF16)
    shared = jnp.dot(a, sd_ref[...], preferred_element_type=F32)

    def mix_slot(slot):
        slot_rows = stage.at[pl.ds(slot * STAGE_ROWS, STAGE_ROWS)]
        _wait_segments(fetch(slot))

        def mix(t, carry):
            j = t * TOP_K
            acc = sw_ref[j] * slot_rows[sp_ref[j]]
            for k in range(1, TOP_K):
                acc = acc + sw_ref[j + k] * slot_rows[sp_ref[j + k]]
            moe_buf[pl.ds(pl.multiple_of(t * SLABS, SLABS), SLABS), :] = acc
            return carry

        lax.fori_loop(0, TM, mix, 0, unroll=2)

    _per_slot(i, mix_slot)
    f = shared + _from_slabs(moe_buf, TM)
    o_ref[...] = x_ref[...] + m_ref[0, 5:6, :] * _rms(f, gp_ref[...])


def _combine(seg_cnt, seg_dst, spos, sw, xtok, mods, h2, ys3, sgu, sd, g_post, latent_only):
    const = lambda shape: pl.BlockSpec(shape, lambda i, c, d: (0, 0))
    skip = CTX_TILES if latent_only else 0
    smem = lambda: pl.BlockSpec((TM * TOP_K,), lambda i, c, d: (i,), memory_space=pltpu.SMEM)
    grid_spec = pltpu.PrefetchScalarGridSpec(
        num_scalar_prefetch=2,
        grid=(N_TILES,),
        in_specs=[smem(), smem(),
                  pl.BlockSpec((TM, D), lambda i, c, d: (i, 0)),
                  pl.BlockSpec((1, 8, D), lambda i, c, d: (_mod_row(i), 0, 0)),
                  pl.BlockSpec((TM * SLABS, LANES), lambda i, c, d: (i, 0)),
                  pl.BlockSpec(memory_space=pl.ANY),
                  const((D, 2 * SDIM)), const((SDIM, D)), const((1, D))],
        out_specs=pl.BlockSpec((TM, D), lambda i, c, d: (jnp.maximum(i - skip, 0), 0)),
        scratch_shapes=[pltpu.VMEM((2 * STAGE_ROWS, SLABS, LANES), F32),
                        pltpu.VMEM((TM * SLABS, LANES), F32),
                        pltpu.SemaphoreType.DMA((2,))],
    )
    return pl.pallas_call(
        _combine_kernel,
        grid_spec=grid_spec,
        out_shape=jax.ShapeDtypeStruct((NT - skip * TM, D), F32),
        compiler_params=_cparams(("arbitrary",)),
        name="combine",
    )(seg_cnt, seg_dst, spos, sw, xtok, mods, h2, ys3, sgu, sd, g_post)


def _moe(xtok, mods, g_pre, g_post, router_w, router_bias, e_g, e_u, e_d, s_g, s_u, s_d, tri, low,
         latent_only):
    h2, spos, slot_w, cnt = _route(
        xtok, mods, g_pre, router_w.T, router_bias.reshape(N_EXP, 1), tri, low)
    tile_cnt = cnt[:, 0].astype(jnp.int32).reshape(N_TILES, N_EXP)
    before = jnp.cumsum(tile_cnt, axis=0) - tile_cnt
    counts = jnp.sum(tile_cnt, axis=0)
    cend = jnp.cumsum(counts)
    cstart = cend - counts
    seg_cnt = tile_cnt.reshape(-1)
    seg_dst = (cstart[None, :] + before).reshape(-1)
    blk_start = jnp.arange(N_BLOCKS, dtype=jnp.int32) * MOE_BLOCK
    inner = jnp.logical_and(counts > 0, cstart % MOE_BLOCK != 0)
    pos = jnp.sort(jnp.concatenate([blk_start, jnp.where(inner, cstart, N_ROWS)]))
    n_items = jnp.sum((pos < N_ROWS).astype(jnp.int32)).reshape(1)
    pos = jnp.minimum(pos, N_ROWS - 1)
    item_blk = pos // MOE_BLOCK
    item_exp = jnp.sum((cend[None, :] <= pos[:, None]).astype(jnp.int32), axis=1)
    is_exp = item_exp[:, None] == jnp.arange(N_EXP, dtype=jnp.int32)[None, :]
    item_end = jnp.sum(jnp.where(is_exp, cend[None, :], 0), axis=1)
    item_lo = pos - item_blk * MOE_BLOCK
    item_hi = jnp.minimum(item_end - item_blk * MOE_BLOCK, MOE_BLOCK)

    spos = spos.T.reshape(-1)
    slot_w = slot_w.T.reshape(-1)
    xs3 = _dispatch(seg_cnt, seg_dst, spos, h2.reshape(NT, SLABS, LANES))
    wgu = jnp.concatenate([e_g, e_u], axis=-1).astype(BF16)
    ys2 = _experts(item_blk, item_exp, item_lo, item_hi, n_items,
                   xs3.reshape(N_ROWS * SLABS, LANES), wgu, e_d.astype(BF16))
    sgu = jnp.concatenate([s_g, s_u], axis=-1).astype(BF16)
    return _combine(seg_cnt, seg_dst, spos, slot_w, xtok, mods, h2,
                    ys2.reshape(N_ROWS, SLABS, LANES), sgu, s_d.astype(BF16), g_post, latent_only)


def kernel(x, c, ctx, c_ctx, w_mod, b_mod, g_pre_mix, g_post_mix, g_pre_ffn, g_post_ffn, w_in, gmlp_ln_g, gmlp_ln_b, gmlp_ws, gmlp_bs, conv_w, conv_b, lru_wa, lru_ba, lru_wx, lru_bx, lru_lambda, w_out_a, w_out_b, w_out, router_w, router_bias, exp_w_gate, exp_w_up, exp_w_down, sh_w_gate, sh_w_up, sh_w_down):
    cond = jnp.concatenate([c_ctx[None, :], c, jnp.zeros((N_COND - 1 - BATCH, D), F32)], axis=0)
    mods_all = _ada(cond, w_mod, b_mod)
    xtok = _entry(ctx.reshape(N_CTX, D), x.reshape(BATCH * SEQ, D), _grid_pos_embed())
    tri = (jnp.arange(TM)[:, None] <= jnp.arange(TM)[None, :]).astype(BF16)
    low = (jnp.arange(N_EXP)[:, None] > jnp.arange(N_EXP)[None, :]).astype(F32)

    for l in range(DEPTH):
        mods = mods_all[l]
        row = lambda p: p[l].reshape(1, -1)
        u, v, xb, gb, ga, gr = _premix(xtok, mods, row(g_pre_mix), w_in[l].astype(BF16))

        hs = []
        for d in range(2):
            wg = jnp.concatenate([_block_diag(lru_wa[l, d]), _block_diag(lru_wx[l, d])], axis=1)
            bg = jnp.concatenate([lru_ba[l, d].reshape(1, LW), lru_bx[l, d].reshape(1, LW)], axis=1)
            hs.append(_lru(xb, conv_w[l], conv_b[l].reshape(1, LW), wg.astype(BF16)[None],
                           bg[None], lru_lambda[l, d].reshape(1, 1, LW), reverse=(d == 1)))

        ws_cat = jnp.transpose(gmlp_ws[l], (1, 0, 2)).reshape(CHUNK, GROUPS * CHUNK).astype(BF16)
        bs_full = jnp.repeat(gmlp_bs[l].T, GW // GROUPS, axis=1)
        xtok = _postmix(xtok, mods, u, v, gb, ga, gr, hs[0], hs[1], row(gmlp_ln_g), row(gmlp_ln_b),
                        ws_cat, bs_full, w_out_a[l].astype(BF16), w_out_b[l].astype(BF16),
                        w_out[l].astype(BF16), row(g_post_mix))

        xtok = _moe(xtok, mods, row(g_pre_ffn), row(g_post_ffn), router_w[l], router_bias[l],
                    exp_w_gate[l], exp_w_up[l], exp_w_down[l],
                    sh_w_gate[l], sh_w_up[l], sh_w_down[l], tri, low, latent_only=(l == DEPTH - 1))

    return xtok.reshape(BATCH, SEQ, D)
```

```python
import functools

import jax
import jax.numpy as jnp
from jax import lax
from jax.experimental import pallas as pl
from jax.experimental.pallas import tpu as pltpu

F32 = jnp.float32
BF16 = jnp.bfloat16

D = 1024
BATCH = 4
SEQ = 8192
DEPTH = 2
CTX = 256
GRID_W = 64
POS_BASE = 10000.0
CHUNK = 128
GROUPS = 8
GW = 512
LW = 512
HEADS = 8
HDIM = LW // HEADS
CONV_W = 4
CONV_LEFT = 2
LRU_C = 8.0
IN_WIDTH = 2 * GW + 2 * LW + 2 * D
N_EXP = 64
TOP_K = 8
N_GRP = 8
TOPK_GRP = 4
EDIM = 256
SDIM = 256
ROUTE_SCALE = 2.5
RMS_EPS = 1e-6
LN_EPS = 1e-5

SUBLANES = 8
LANES = 128
SLABS = D // LANES
VMEM_LIMIT = 56 * 1024 * 1024

N_CTX = BATCH * CTX
NT = N_CTX + BATCH * SEQ
TM = 512
N_TILES = NT // TM
CTX_TILES = N_CTX // TM
TILES_PER_SEQ = SEQ // TM
TL = CTX
LRU_STEPS = 1 + SEQ // TL
MOE_BLOCK = 512
MOE_SUB = 512
N_ROWS = NT * TOP_K
N_BLOCKS = N_ROWS // MOE_BLOCK
N_ITEMS = N_BLOCKS + N_EXP
N_COND = 8


def _cparams(sem):
    return pltpu.CompilerParams(dimension_semantics=sem, vmem_limit_bytes=VMEM_LIMIT)


def _mod_row(i):
    return jnp.where(i < CTX_TILES, 0, 1 + (i - CTX_TILES) // TILES_PER_SEQ)


def _rms(x, g):
    return x * lax.rsqrt(jnp.mean(x * x, axis=-1, keepdims=True) + RMS_EPS) * g


def _to_slabs(ref, val, row0=0):
    rows = val.shape[0]
    for s in range(SLABS):
        ref[pl.ds(row0 * SLABS + s, rows, stride=SLABS), :] = val[:, s * LANES:(s + 1) * LANES]


def _from_slabs(ref, rows, row0=0):
    return jnp.concatenate(
        [ref[pl.ds(row0 * SLABS + s, rows, stride=SLABS), :] for s in range(SLABS)], axis=1)


def _ada_kernel(c_ref, w_ref, b_ref, o_ref):
    a = jax.nn.silu(c_ref[...])
    o_ref[0, 0] = jnp.dot(a, w_ref[0], precision=lax.Precision.HIGHEST,
                          preferred_element_type=F32) + b_ref[0, 0]


def _ada(cond, w_mod, b_mod):
    out = pl.pallas_call(
        _ada_kernel,
        grid=(DEPTH, 6),
        in_specs=[pl.BlockSpec((N_COND, D), lambda l, j: (0, 0)),
                  pl.BlockSpec((1, D, D), lambda l, j: (l, 0, j)),
                  pl.BlockSpec((1, 1, 1, D), lambda l, j: (l, j, 0, 0))],
        out_specs=pl.BlockSpec((1, 1, N_COND, D), lambda l, j: (l, j, 0, 0)),
        out_shape=jax.ShapeDtypeStruct((DEPTH, 6, N_COND, D), F32),
        compiler_params=_cparams(("parallel", "parallel")),
        name="ada_params",
    )(cond, w_mod, b_mod.reshape(DEPTH, 6, 1, D))
    mods = jnp.transpose(out, (0, 2, 1, 3))
    return jnp.pad(mods, ((0, 0), (0, 0), (0, 2), (0, 0)))


def _entry_kernel(ctx_ref, x_ref, pe_ref, o_ref):
    i = pl.program_id(0)

    @pl.when(i < CTX_TILES)
    def _():
        o_ref[...] = ctx_ref[...]

    @pl.when(i >= CTX_TILES)
    def _():
        o_ref[...] = x_ref[...] + pe_ref[...]


def _entry(ctx2, x2, pe):
    return pl.pallas_call(
        _entry_kernel,
        grid=(N_TILES,),
        in_specs=[pl.BlockSpec((TM, D), lambda i: (jnp.minimum(i, CTX_TILES - 1), 0)),
                  pl.BlockSpec((TM, D), lambda i: (jnp.maximum(i - CTX_TILES, 0), 0)),
                  pl.BlockSpec((TM, D), lambda i: (jnp.maximum(i - CTX_TILES, 0) % TILES_PER_SEQ, 0))],
        out_specs=pl.BlockSpec((TM, D), lambda i: (i, 0)),
        out_shape=jax.ShapeDtypeStruct((NT, D), F32),
        compiler_params=_cparams(("parallel",)),
        name="entry",
    )(ctx2, x2, pe)


def _grid_pos_embed():
    rows = SEQ // GRID_W
    quarter = D // 4
    half = D // 2
    omega = 1.0 / (POS_BASE ** (jnp.arange(quarter, dtype=F32) / quarter))

    def sincos(p):
        ang = p[:, None] * omega[None, :]
        return jnp.concatenate([jnp.sin(ang), jnp.cos(ang)], axis=-1)

    row_e = sincos(jnp.arange(rows, dtype=F32))
    col_e = sincos(jnp.arange(GRID_W, dtype=F32))
    pe = jnp.concatenate([jnp.broadcast_to(row_e[:, None, :], (rows, GRID_W, half)),
                          jnp.broadcast_to(col_e[None, :, :], (rows, GRID_W, half))], axis=-1)
    return pe.reshape(SEQ, D)


def _premix_kernel(x_ref, m_ref, g_ref, w_ref, u_ref, v_ref, xb_ref, gb_ref, ga_ref, gr_ref):
    x = x_ref[...]
    h = _rms(x, g_ref[...]) * (1.0 + m_ref[0, 1:2, :]) + m_ref[0, 0:1, :]
    hb = h.astype(BF16)

    def proj(lo, width):
        return jnp.dot(hb, w_ref[:, lo:lo + width], preferred_element_type=F32)

    u_ref[...] = jax.nn.gelu(proj(0, GW)).astype(BF16)
    v_ref[...] = jax.nn.gelu(proj(GW, GW)).astype(BF16)
    xb_ref[...] = proj(2 * GW, LW)
    gb_ref[...] = jax.nn.gelu(proj(2 * GW + LW, LW)).astype(BF16)
    base = 2 * GW + 2 * LW
    for j in range(2):
        ga_ref[:, j * 512:(j + 1) * 512] = jax.nn.sigmoid(proj(base + j * 512, 512)).astype(BF16)
        gr_ref[:, j * 512:(j + 1) * 512] = jax.nn.sigmoid(proj(base + D + j * 512, 512)).astype(BF16)


def _premix(xtok, mods, g_pre, w_in_bf):
    tile = lambda w: pl.BlockSpec((TM, w), lambda i: (i, 0))
    return pl.pallas_call(
        _premix_kernel,
        grid=(N_TILES,),
        in_specs=[tile(D),
                  pl.BlockSpec((1, 8, D), lambda i: (_mod_row(i), 0, 0)),
                  pl.BlockSpec((1, D), lambda i: (0, 0)),
                  pl.BlockSpec((D, IN_WIDTH), lambda i: (0, 0))],
        out_specs=[tile(GW), tile(GW), tile(LW), tile(LW), tile(D), tile(D)],
        out_shape=[jax.ShapeDtypeStruct((NT, GW), BF16), jax.ShapeDtypeStruct((NT, GW), BF16),
                   jax.ShapeDtypeStruct((NT, LW), F32), jax.ShapeDtypeStruct((NT, LW), BF16),
                   jax.ShapeDtypeStruct((NT, D), BF16), jax.ShapeDtypeStruct((NT, D), BF16)],
        compiler_params=_cparams(("parallel",)),
        name="premix",
    )(xtok, mods, g_pre, w_in_bf)


def _lru_block(b, j, reverse):
    k = (LRU_STEPS - 1 - j) if reverse else (j - 1)
    return jnp.where(j == 0, b, BATCH + b * (SEQ // TL) + k), k


def _lru_kernel(x_ref, xp_ref, xn_ref, cw_ref, cb_ref, wg_ref, bg_ref, lam_ref, h_ref,
                xpad, a_buf, d_buf, carry, *, reverse):
    j = pl.program_id(1)
    k = (LRU_STEPS - 1 - j) if reverse else (j - 1)
    has_prev = jnp.logical_and(j > 0, k >= 1)
    has_next = jnp.logical_and(j > 0, k <= SEQ // TL - 2)

    @pl.when(j == 0)
    def _():
        carry[...] = jnp.zeros_like(carry)

    xpad[0:SUBLANES, :] = jnp.where(has_prev, xp_ref[...], 0.0)
    xpad[SUBLANES:SUBLANES + TL, :] = x_ref[...]
    xpad[SUBLANES + TL:, :] = jnp.where(has_next, xn_ref[...], 0.0)
    xc = cb_ref[...]
    for tap in range(CONV_W):
        off = SUBLANES - CONV_LEFT + tap
        xc = xc + xpad[off:off + TL, :] * cw_ref[tap:tap + 1, :]

    gates = jnp.dot(xc.astype(BF16), wg_ref[0], preferred_element_type=F32) + bg_ref[0]
    r = 0.5 * jnp.tanh(0.5 * gates[:, :LW]) + 0.5
    ig = 0.5 * jnp.tanh(0.5 * gates[:, LW:]) + 0.5
    lam = lam_ref[0]
    softplus_neg = jnp.maximum(-lam, 0.0) + jnp.log1p(jnp.exp(-jnp.abs(lam)))
    log_a = -LRU_C * r * softplus_neg
    a_buf[...] = jnp.exp(log_a)
    th = jnp.tanh(log_a)
    d_buf[...] = jnp.sqrt(-2.0 * th / (1.0 - th)) * (ig * xc)

    row = lax.broadcasted_iota(jnp.int32, (SUBLANES, LW), 0)
    n_grp = TL // SUBLANES

    def body(g, c):
        gi = (n_grp - 1 - g) if reverse else g
        sl = pl.ds(pl.multiple_of(gi * SUBLANES, SUBLANES), SUBLANES)
        a = a_buf[sl, :]
        d = d_buf[sl, :]
        for s in (1, 2, 4):
            shift = (SUBLANES - s) if reverse else s
            m = (row < SUBLANES - s) if reverse else (row >= s)
            d = jnp.where(m, a * pltpu.roll(d, shift, 0) + d, d)
            a = jnp.where(m, a * pltpu.roll(a, shift, 0), a)
        h = d + a * c
        h_ref[sl, :] = h
        edge = 0 if reverse else SUBLANES - 1
        return jnp.broadcast_to(h[edge:edge + 1, :], (SUBLANES, LW))

    carry[...] = lax.fori_loop(0, n_grp, body, carry[...], unroll=4)


def _lru(xb, conv_w, conv_b, wg, bg, lam, reverse):
    n8 = NT // SUBLANES
    per8 = TL // SUBLANES

    def cur(b, j):
        return (_lru_block(b, j, reverse)[0], 0)

    def prev(b, j):
        return (jnp.maximum(_lru_block(b, j, reverse)[0] * per8 - 1, 0), 0)

    def nxt(b, j):
        return (jnp.minimum(_lru_block(b, j, reverse)[0] * per8 + per8, n8 - 1), 0)

    const2 = lambda shape: pl.BlockSpec(shape, lambda b, j: (0, 0))
    const3 = lambda shape: pl.BlockSpec(shape, lambda b, j: (0, 0, 0))
    return pl.pallas_call(
        functools.partial(_lru_kernel, reverse=reverse),
        grid=(BATCH, LRU_STEPS),
        in_specs=[pl.BlockSpec((TL, LW), cur),
                  pl.BlockSpec((SUBLANES, LW), prev),
                  pl.BlockSpec((SUBLANES, LW), nxt),
                  const2((CONV_W, LW)), const2((1, LW)),
                  const3((1, LW, 2 * LW)), const3((1, 1, 2 * LW)), const3((1, 1, LW))],
        out_specs=pl.BlockSpec((TL, LW), cur),
        out_shape=jax.ShapeDtypeStruct((NT, LW), F32),
        scratch_shapes=[pltpu.VMEM((TL + 2 * SUBLANES, LW), F32),
                        pltpu.VMEM((TL, LW), F32), pltpu.VMEM((TL, LW), F32),
                        pltpu.VMEM((SUBLANES, LW), F32)],
        compiler_params=_cparams(("arbitrary", "arbitrary")),
        name="lru_bwd" if reverse else "lru_fwd",
    )(xb, xb, xb, conv_w, conv_b, wg, bg, lam)


def _block_diag(w):
    eye = jnp.eye(HEADS, dtype=w.dtype)
    return jnp.einsum('hij,hg->higj', w, eye).reshape(LW, LW)


def _postmix_kernel(x_ref, m_ref, u_ref, v_ref, gb_ref, ga_ref, gr_ref, hf_ref, hb_ref,
                    lng_ref, lnb_ref, ws_ref, bs_ref, woa_ref, wob_ref, wo_ref, gp_ref, o_ref,
                    mix_buf):
    v = v_ref[...].astype(F32)
    mu = jnp.mean(v, axis=-1, keepdims=True)
    var = jnp.mean(jnp.square(v - mu), axis=-1, keepdims=True)
    vn = ((v - mu) * lax.rsqrt(var + LN_EPS) * lng_ref[...] + lnb_ref[...]).astype(BF16)
    col_grp = lax.broadcasted_iota(jnp.int32, (CHUNK, GW), 1) // (GW // GROUPS)
    for c in range(TM // CHUNK):
        vc = vn[c * CHUNK:(c + 1) * CHUNK, :]
        stacked = jnp.concatenate(
            [jnp.where(col_grp == g, vc, jnp.zeros_like(vc)) for g in range(GROUPS)], axis=0)
        mix_buf[c * CHUNK:(c + 1) * CHUNK, :] = (
            jnp.dot(ws_ref[...], stacked, preferred_element_type=F32) + bs_ref[...])
    ya = (u_ref[...].astype(F32) * mix_buf[...]).astype(BF16)
    za = jnp.dot(ya, woa_ref[...], preferred_element_type=F32)
    yb = ((hf_ref[...] + hb_ref[...]) * gb_ref[...].astype(F32)).astype(BF16)
    zb = jnp.dot(yb, wob_ref[...], preferred_element_type=F32)
    m = (ga_ref[...].astype(F32) * za + gr_ref[...].astype(F32) * zb).astype(BF16)
    y = jnp.dot(m, wo_ref[...], preferred_element_type=F32)
    o_ref[...] = x_ref[...] + m_ref[0, 2:3, :] * _rms(y, gp_ref[...])


def _postmix(xtok, mods, u, v, gb, ga, gr, hf, hb, ln_g, ln_b, ws_cat, bs_full, woa, wob, wo, g_post):
    tile = lambda w: pl.BlockSpec((TM, w), lambda i: (i, 0))
    const = lambda shape: pl.BlockSpec(shape, lambda i: (0, 0))
    return pl.pallas_call(
        _postmix_kernel,
        grid=(N_TILES,),
        in_specs=[tile(D), pl.BlockSpec((1, 8, D), lambda i: (_mod_row(i), 0, 0)),
                  tile(GW), tile(GW), tile(LW), tile(D), tile(D), tile(LW), tile(LW),
                  const((1, GW)), const((1, GW)), const((CHUNK, GROUPS * CHUNK)), const((CHUNK, GW)),
                  const((GW, D)), const((LW, D)), const((D, D)), const((1, D))],
        out_specs=tile(D),
        out_shape=jax.ShapeDtypeStruct((NT, D), F32),
        scratch_shapes=[pltpu.VMEM((TM, GW), F32)],
        compiler_params=_cparams(("parallel",)),
        name="postmix",
    )(xtok, mods, u, v, gb, ga, gr, hf, hb, ln_g, ln_b, ws_cat, bs_full, woa, wob, wo, g_post)


def _route_kernel(x_ref, m_ref, g_ref, rw_ref, rb_ref, tri_ref, low_ref,
                  h_ref, sp_ref, sw_ref, cnt_ref):
    h = _rms(x_ref[...], g_ref[...]) * (1.0 + m_ref[0, 4:5, :]) + m_ref[0, 3:4, :]
    _to_slabs(h_ref, h)

    logits = lax.dot_general(rw_ref[...], h, (((1,), (1,)), ((), ())),
                             precision=lax.Precision.HIGHEST, preferred_element_type=F32)
    s = jax.nn.sigmoid(logits)
    sel = s + rb_ref[...]
    per_grp = N_EXP // N_GRP
    sel3 = sel.reshape(N_GRP, per_grp, TM)
    m1 = jnp.max(sel3, axis=1)
    is_m1 = sel3 == m1[:, None, :]
    n_m1 = jnp.sum(is_m1.astype(F32), axis=1)
    m2 = jnp.max(jnp.where(is_m1, -jnp.inf, sel3), axis=1)
    grp_score = m1 + jnp.where(n_m1 >= 2.0, m1, m2)

    g_iota = lax.broadcasted_iota(jnp.int32, (N_GRP, TM), 0)
    g_rank = jnp.zeros((N_GRP, TM), F32)
    for g in range(N_GRP):
        other = grp_score[g:g + 1, :]
        g_rank = g_rank + jnp.where(other > grp_score, 1.0, 0.0)
        g_rank = g_rank + jnp.where(jnp.logical_and(other == grp_score, g_iota > g), 1.0, 0.0)
    g_keep = jnp.where(g_rank < TOPK_GRP, 1.0, 0.0)
    e_keep = jnp.broadcast_to(g_keep[:, None, :], (N_GRP, per_grp, TM)).reshape(N_EXP, TM)
    selm = jnp.where(e_keep > 0.0, sel, -jnp.inf)

    e_iota = lax.broadcasted_iota(jnp.int32, (N_EXP, TM), 0)
    e_rank = jnp.zeros((N_EXP, TM), F32)
    for e in range(N_EXP):
        other = selm[e:e + 1, :]
        e_rank = e_rank + jnp.where(other > selm, 1.0, 0.0)
        e_rank = e_rank + jnp.where(jnp.logical_and(other == selm, e_iota > e), 1.0, 0.0)
    chosen = e_rank < TOP_K
    sc = jnp.where(chosen, s, 0.0)
    w = sc / jnp.sum(sc, axis=0, keepdims=True) * ROUTE_SCALE

    cum = jnp.dot(jnp.where(chosen, 1.0, 0.0).astype(BF16), tri_ref[...],
                  preferred_element_type=F32)
    tile_cnt = jnp.broadcast_to(cum[:, TM - 1:TM], (N_EXP, LANES))
    cnt_ref[...] = tile_cnt
    tile_off = jnp.dot(low_ref[...], tile_cnt, precision=lax.Precision.HIGHEST,
                       preferred_element_type=F32)[:, 0:1]
    pos = tile_off + cum - 1.0

    for r in range(TOP_K):
        hit = e_rank == float(r)
        sp_ref[r:r + 1, :] = jnp.sum(jnp.where(hit, pos, 0.0), axis=0, keepdims=True).astype(jnp.int32)
        sw_ref[r:r + 1, :] = jnp.sum(jnp.where(hit, w, 0.0), axis=0, keepdims=True)


def _route(xtok, mods, g_pre, rw_t, rb, tri, low):
    const = lambda shape: pl.BlockSpec(shape, lambda i: (0, 0))
    slot = pl.BlockSpec((TOP_K, TM), lambda i: (0, i))
    return pl.pallas_call(
        _route_kernel,
        grid=(N_TILES,),
        in_specs=[pl.BlockSpec((TM, D), lambda i: (i, 0)),
                  pl.BlockSpec((1, 8, D), lambda i: (_mod_row(i), 0, 0)),
                  const((1, D)), const((N_EXP, D)), const((N_EXP, 1)), const((TM, TM)),
                  const((N_EXP, N_EXP))],
        out_specs=[pl.BlockSpec((TM * SLABS, LANES), lambda i: (i, 0)), slot, slot,
                   pl.BlockSpec((N_EXP, LANES), lambda i: (i, 0))],
        out_shape=[jax.ShapeDtypeStruct((NT * SLABS, LANES), F32),
                   jax.ShapeDtypeStruct((TOP_K, NT), jnp.int32),
                   jax.ShapeDtypeStruct((TOP_K, NT), F32),
                   jax.ShapeDtypeStruct((N_TILES * N_EXP, LANES), F32)],
        compiler_params=_cparams(("parallel",)),
        name="route",
    )(xtok, mods, g_pre, rw_t, rb, tri, low)


SEG_BITS = TM.bit_length()


def _start_segments(i, cnt_ref, dst_ref, make_copy):
    def seg(e, off):
        r = cnt_ref[i * N_EXP + e]
        d = dst_ref[i * N_EXP + e]
        for b in range(SEG_BITS):
            n = 1 << b

            @pl.when(((r >> b) & 1) == 1)
            def _():
                o = r & (n - 1)
                make_copy(off + o, d + o, n).start()

        return off + r

    lax.fori_loop(0, N_EXP, seg, 0)


def _wait_segments(make_copy):
    for _ in range(TOP_K):
        make_copy(0, 0, TM).wait()


def _per_slot(i, fn):
    for slot in range(2):
        @pl.when(i % 2 == slot)
        def _():
            fn(slot)


STAGE_ROWS = TOP_K * TM


def _dispatch_kernel(cnt_ref, dst_ref, sp_ref, h_ref, xs_ref, stage, sems):
    i = pl.program_id(0)

    def step(slot):
        base = slot * STAGE_ROWS

        def make_copy(src, dst, n):
            return pltpu.make_async_copy(stage.at[pl.ds(base + src, n)], xs_ref.at[pl.ds(dst, n)],
                                         sems.at[slot])

        @pl.when(i >= 2)
        def _():
            _wait_segments(make_copy)

        slot_rows = stage.at[pl.ds(base, STAGE_ROWS)]

        def place(t, carry):
            slab = h_ref[t]
            for k in range(TOP_K):
                slot_rows[sp_ref[t * TOP_K + k]] = slab
            return carry

        lax.fori_loop(0, TM, place, 0, unroll=2)
        _start_segments(i, cnt_ref, dst_ref, make_copy)

        @pl.when(i == N_TILES - 1)
        def _():
            _wait_segments(make_copy)

    _per_slot(i, step)

    @pl.when(i == N_TILES - 1)
    def _():
        other = (N_TILES - 2) % 2

        def other_copy(src, dst, n):
            return pltpu.make_async_copy(stage.at[pl.ds(other * STAGE_ROWS + src, n)],
                                         xs_ref.at[pl.ds(dst, n)], sems.at[other])

        _wait_segments(other_copy)


def _dispatch(seg_cnt, seg_dst, spos_flat, h3):
    grid_spec = pltpu.PrefetchScalarGridSpec(
        num_scalar_prefetch=2,
        grid=(N_TILES,),
        in_specs=[pl.BlockSpec((TM * TOP_K,), lambda i, c, d: (i,), memory_space=pltpu.SMEM),
                  pl.BlockSpec((TM, SLABS, LANES), lambda i, c, d: (i, 0, 0))],
        out_specs=pl.BlockSpec(memory_space=pl.ANY),
        scratch_shapes=[pltpu.VMEM((2 * STAGE_ROWS, SLABS, LANES), F32),
                        pltpu.SemaphoreType.DMA((2,))],
    )
    return pl.pallas_call(
        _dispatch_kernel,
        grid_spec=grid_spec,
        out_shape=jax.ShapeDtypeStruct((N_ROWS, SLABS, LANES), F32),
        compiler_params=_cparams(("arbitrary",)),
        name="dispatch",
    )(seg_cnt, seg_dst, spos_flat, h3)


def _expert_kernel(ib_ref, ie_ref, lo_ref, hi_ref, ni_ref, x_ref, wg_ref, wu_ref, wd_ref, y_ref,
                   wgu_bf, wd_bf):
    j = pl.program_id(0)

    def swiglu(r0):
        x = _from_slabs(x_ref, MOE_SUB, r0).astype(BF16)
        gu = jnp.dot(x, wgu_bf[...], preferred_element_type=F32)
        a = (jax.nn.silu(gu[:, :EDIM]) * gu[:, EDIM:]).astype(BF16)
        return jnp.dot(a, wd_bf[...], preferred_element_type=F32)

    @pl.when(j < ni_ref[0])
    def _():
        prev = jnp.maximum(j - 1, 0)

        @pl.when(jnp.logical_or(j == 0, ie_ref[j] != ie_ref[prev]))
        def _():
            wgu_bf[:, :EDIM] = wg_ref[0, 0].astype(BF16)
            wgu_bf[:, EDIM:] = wu_ref[0, 0].astype(BF16)
            wd_bf[...] = wd_ref[0, 0].astype(BF16)

        first = jnp.logical_or(j == 0, ib_ref[j] != ib_ref[prev])

        @pl.when(first)
        def _():
            for r0 in range(0, MOE_BLOCK, MOE_SUB):
                _to_slabs(y_ref, swiglu(r0), r0)

        @pl.when(jnp.logical_not(first))
        def _():
            for r0 in range(0, MOE_BLOCK, MOE_SUB):
                row = r0 + lax.broadcasted_iota(jnp.int32, (MOE_SUB, D), 0)
                mine = jnp.logical_and(row >= lo_ref[j], row < hi_ref[j])
                _to_slabs(y_ref, jnp.where(mine, swiglu(r0), _from_slabs(y_ref, MOE_SUB, r0)), r0)


def _experts(layer, item_blk, item_exp, item_lo, item_hi, n_items, xs2, w_gate, w_up, w_down):
    def rows(j, ib, ie, lo, hi, ni):
        return (ib[jnp.minimum(j, ni[0] - 1)], 0)

    def wsel(j, ib, ie, lo, hi, ni):
        return (layer, ie[jnp.minimum(j, ni[0] - 1)], 0, 0)

    grid_spec = pltpu.PrefetchScalarGridSpec(
        num_scalar_prefetch=5,
        grid=(N_ITEMS,),
        in_specs=[pl.BlockSpec((MOE_BLOCK * SLABS, LANES), rows),
                  pl.BlockSpec((1, 1, D, EDIM), wsel),
                  pl.BlockSpec((1, 1, D, EDIM), wsel),
                  pl.BlockSpec((1, 1, EDIM, D), wsel)],
        out_specs=pl.BlockSpec((MOE_BLOCK * SLABS, LANES), rows),
        scratch_shapes=[pltpu.VMEM((D, 2 * EDIM), BF16), pltpu.VMEM((EDIM, D), BF16)],
    )
    return pl.pallas_call(
        _expert_kernel,
        grid_spec=grid_spec,
        out_shape=jax.ShapeDtypeStruct((N_ROWS * SLABS, LANES), F32),
        compiler_params=_cparams(("arbitrary",)),
        name="experts",
    )(item_blk, item_exp, item_lo, item_hi, n_items, xs2, w_gate, w_up, w_down)


def _combine_kernel(cnt_ref, dst_ref, sp_ref, sw_ref, x_ref, m_ref, h_ref, ys_ref, sgu_ref, sd_ref,
                    gp_ref, o_ref, stage, moe_buf, sems):
    i = pl.program_id(0)

    def fetch(slot):
        def make_copy(dst, src, n):
            return pltpu.make_async_copy(ys_ref.at[pl.ds(src, n)],
                                         stage.at[pl.ds(slot * STAGE_ROWS + dst, n)], sems.at[slot])
        return make_copy

    @pl.when(i == 0)
    def _():
        _start_segments(0, cnt_ref, dst_ref, fetch(0))

    @pl.when(i + 1 < N_TILES)
    def _():
        _per_slot(i + 1, lambda slot: _start_segments(i + 1, cnt_ref, dst_ref, fetch(slot)))

    hb = _from_slabs(h_ref, TM).astype(BF16)
    gu = jnp.dot(hb, sgu_ref[...], preferred_element_type=F32)
    a = (jax.nn.silu(gu[:, :SDIM]) * gu[:, SDIM:]).astype(BF16)
    shared = jnp.dot(a, sd_ref[...], preferred_element_type=F32)

    def mix_slot(slot):
        slot_rows = stage.at[pl.ds(slot * STAGE_ROWS, STAGE_ROWS)]
        _wait_segments(fetch(slot))

        def mix(t, carry):
            j = t * TOP_K
            acc = sw_ref[j] * slot_rows[sp_ref[j]]
            for k in range(1, TOP_K):
                acc = acc + sw_ref[j + k] * slot_rows[sp_ref[j + k]]
            moe_buf[pl.ds(pl.multiple_of(t * SLABS, SLABS), SLABS), :] = acc
            return carry

        lax.fori_loop(0, TM, mix, 0, unroll=2)

    _per_slot(i, mix_slot)
    f = shared + _from_slabs(moe_buf, TM)
    o_ref[...] = x_ref[...] + m_ref[0, 5:6, :] * _rms(f, gp_ref[...])


def _combine(seg_cnt, seg_dst, spos, sw, xtok, mods, h2, ys3, sgu, sd, g_post, latent_only):
    const = lambda shape: pl.BlockSpec(shape, lambda i, c, d: (0, 0))
    skip = CTX_TILES if latent_only else 0
    smem = lambda: pl.BlockSpec((TM * TOP_K,), lambda i, c, d: (i,), memory_space=pltpu.SMEM)
    grid_spec = pltpu.PrefetchScalarGridSpec(
        num_scalar_prefetch=2,
        grid=(N_TILES,),
        in_specs=[smem(), smem(),
                  pl.BlockSpec((TM, D), lambda i, c, d: (i, 0)),
                  pl.BlockSpec((1, 8, D), lambda i, c, d: (_mod_row(i), 0, 0)),
                  pl.BlockSpec((TM * SLABS, LANES), lambda i, c, d: (i, 0)),
                  pl.BlockSpec(memory_space=pl.ANY),
                  const((D, 2 * SDIM)), const((SDIM, D)), const((1, D))],
        out_specs=pl.BlockSpec((TM, D), lambda i, c, d: (jnp.maximum(i - skip, 0), 0)),
        scratch_shapes=[pltpu.VMEM((2 * STAGE_ROWS, SLABS, LANES), F32),
                        pltpu.VMEM((TM * SLABS, LANES), F32),
                        pltpu.SemaphoreType.DMA((2,))],
    )
    return pl.pallas_call(
        _combine_kernel,
        grid_spec=grid_spec,
        out_shape=jax.ShapeDtypeStruct((NT - skip * TM, D), F32),
        compiler_params=_cparams(("arbitrary",)),
        name="combine",
    )(seg_cnt, seg_dst, spos, sw, xtok, mods, h2, ys3, sgu, sd, g_post)


def _moe(layer, xtok, mods, g_pre, g_post, router_w, router_bias, e_g, e_u, e_d, s_g, s_u, s_d, tri, low,
         latent_only):
    h2, spos, slot_w, cnt = _route(
        xtok, mods, g_pre, router_w.T, router_bias.reshape(N_EXP, 1), tri, low)
    tile_cnt = cnt[:, 0].astype(jnp.int32).reshape(N_TILES, N_EXP)
    before = jnp.cumsum(tile_cnt, axis=0) - tile_cnt
    counts = jnp.sum(tile_cnt, axis=0)
    cend = jnp.cumsum(counts)
    cstart = cend - counts
    seg_cnt = tile_cnt.reshape(-1)
    seg_dst = (cstart[None, :] + before).reshape(-1)
    blk_start = jnp.arange(N_BLOCKS, dtype=jnp.int32) * MOE_BLOCK
    inner = jnp.logical_and(counts > 0, cstart % MOE_BLOCK != 0)
    pos = jnp.sort(jnp.concatenate([blk_start, jnp.where(inner, cstart, N_ROWS)]))
    n_items = jnp.sum((pos < N_ROWS).astype(jnp.int32)).reshape(1)
    pos = jnp.minimum(pos, N_ROWS - 1)
    item_blk = pos // MOE_BLOCK
    item_exp = jnp.sum((cend[None, :] <= pos[:, None]).astype(jnp.int32), axis=1)
    is_exp = item_exp[:, None] == jnp.arange(N_EXP, dtype=jnp.int32)[None, :]
    item_end = jnp.sum(jnp.where(is_exp, cend[None, :], 0), axis=1)
    item_lo = pos - item_blk * MOE_BLOCK
    item_hi = jnp.minimum(item_end - item_blk * MOE_BLOCK, MOE_BLOCK)

    spos = spos.T.reshape(-1)
    slot_w = slot_w.T.reshape(-1)
    xs3 = _dispatch(seg_cnt, seg_dst, spos, h2.reshape(NT, SLABS, LANES))
    ys2 = _experts(layer, item_blk, item_exp, item_lo, item_hi, n_items,
                   xs3.reshape(N_ROWS * SLABS, LANES), e_g, e_u, e_d)
    sgu = jnp.concatenate([s_g, s_u], axis=-1).astype(BF16)
    return _combine(seg_cnt, seg_dst, spos, slot_w, xtok, mods, h2,
                    ys2.reshape(N_ROWS, SLABS, LANES), sgu, s_d.astype(BF16), g_post, latent_only)


def kernel(x, c, ctx, c_ctx, w_mod, b_mod, g_pre_mix, g_post_mix, g_pre_ffn, g_post_ffn, w_in, gmlp_ln_g, gmlp_ln_b, gmlp_ws, gmlp_bs, conv_w, conv_b, lru_wa, lru_ba, lru_wx, lru_bx, lru_lambda, w_out_a, w_out_b, w_out, router_w, router_bias, exp_w_gate, exp_w_up, exp_w_down, sh_w_gate, sh_w_up, sh_w_down):
    cond = jnp.concatenate([c_ctx[None, :], c, jnp.zeros((N_COND - 1 - BATCH, D), F32)], axis=0)
    mods_all = _ada(cond, w_mod, b_mod)
    xtok = _entry(ctx.reshape(N_CTX, D), x.reshape(BATCH * SEQ, D), _grid_pos_embed())
    tri = (jnp.arange(TM)[:, None] <= jnp.arange(TM)[None, :]).astype(BF16)
    low = (jnp.arange(N_EXP)[:, None] > jnp.arange(N_EXP)[None, :]).astype(F32)

    for l in range(DEPTH):
        mods = mods_all[l]
        row = lambda p: p[l].reshape(1, -1)
        u, v, xb, gb, ga, gr = _premix(xtok, mods, row(g_pre_mix), w_in[l].astype(BF16))

        hs = []
        for d in range(2):
            wg = jnp.concatenate([_block_diag(lru_wa[l, d]), _block_diag(lru_wx[l, d])], axis=1)
            bg = jnp.concatenate([lru_ba[l, d].reshape(1, LW), lru_bx[l, d].reshape(1, LW)], axis=1)
            hs.append(_lru(xb, conv_w[l], conv_b[l].reshape(1, LW), wg.astype(BF16)[None],
                           bg[None], lru_lambda[l, d].reshape(1, 1, LW), reverse=(d == 1)))

        ws_cat = jnp.transpose(gmlp_ws[l], (1, 0, 2)).reshape(CHUNK, GROUPS * CHUNK).astype(BF16)
        bs_full = jnp.repeat(gmlp_bs[l].T, GW // GROUPS, axis=1)
        xtok = _postmix(xtok, mods, u, v, gb, ga, gr, hs[0], hs[1], row(gmlp_ln_g), row(gmlp_ln_b),
                        ws_cat, bs_full, w_out_a[l].astype(BF16), w_out_b[l].astype(BF16),
                        w_out[l].astype(BF16), row(g_post_mix))

        xtok = _moe(l, xtok, mods, row(g_pre_ffn), row(g_post_ffn), router_w[l], router_bias[l],
                    exp_w_gate, exp_w_up, exp_w_down,
                    sh_w_gate[l], sh_w_up[l], sh_w_down[l], tri, low, latent_only=(l == DEPTH - 1))

    return xtok.reshape(BATCH, SEQ, D)
```

```python
import functools

import jax
import jax.numpy as jnp
from jax import lax
from jax.experimental import pallas as pl
from jax.experimental.pallas import tpu as pltpu

F32 = jnp.float32
BF16 = jnp.bfloat16

D = 1024
BATCH = 4
SEQ = 8192
DEPTH = 2
CTX = 256
GRID_W = 64
POS_BASE = 10000.0
CHUNK = 128
GROUPS = 8
GW = 512
LW = 512
HEADS = 8
HDIM = LW // HEADS
CONV_W = 4
CONV_LEFT = 2
LRU_C = 8.0
IN_WIDTH = 2 * GW + 2 * LW + 2 * D
N_EXP = 64
TOP_K = 8
N_GRP = 8
TOPK_GRP = 4
EDIM = 256
SDIM = 256
ROUTE_SCALE = 2.5
RMS_EPS = 1e-6
LN_EPS = 1e-5

SUBLANES = 8
LANES = 128
SLABS = D // LANES
VMEM_LIMIT = 56 * 1024 * 1024

N_CTX = BATCH * CTX
NT = N_CTX + BATCH * SEQ
TM = 512
N_TILES = NT // TM
CTX_TILES = N_CTX // TM
TILES_PER_SEQ = SEQ // TM
TL = CTX
LRU_STEPS = 1 + SEQ // TL
MOE_BLOCK = 512
MOE_SUB = 512
N_ROWS = NT * TOP_K
N_BLOCKS = N_ROWS // MOE_BLOCK
N_ITEMS = N_BLOCKS + N_EXP
N_COND = 8


def _cparams(sem):
    return pltpu.CompilerParams(dimension_semantics=sem, vmem_limit_bytes=VMEM_LIMIT)


def _mod_row(i):
    return jnp.where(i < CTX_TILES, 0, 1 + (i - CTX_TILES) // TILES_PER_SEQ)


def _rms(x, g):
    return x * lax.rsqrt(jnp.mean(x * x, axis=-1, keepdims=True) + RMS_EPS) * g


def _to_slabs(ref, val, row0=0):
    rows = val.shape[0]
    for s in range(SLABS):
        ref[pl.ds(row0 * SLABS + s, rows, stride=SLABS), :] = val[:, s * LANES:(s + 1) * LANES]


def _from_slabs(ref, rows, row0=0):
    return jnp.concatenate(
        [ref[pl.ds(row0 * SLABS + s, rows, stride=SLABS), :] for s in range(SLABS)], axis=1)


def _ada_kernel(c_ref, w_ref, b_ref, o_ref):
    a = jax.nn.silu(c_ref[...])
    o_ref[0, 0] = jnp.dot(a, w_ref[0], precision=lax.Precision.HIGHEST,
                          preferred_element_type=F32) + b_ref[0, 0]


def _ada(cond, w_mod, b_mod):
    out = pl.pallas_call(
        _ada_kernel,
        grid=(DEPTH, 6),
        in_specs=[pl.BlockSpec((N_COND, D), lambda l, j: (0, 0)),
                  pl.BlockSpec((1, D, D), lambda l, j: (l, 0, j)),
                  pl.BlockSpec((1, 1, 1, D), lambda l, j: (l, j, 0, 0))],
        out_specs=pl.BlockSpec((1, 1, N_COND, D), lambda l, j: (l, j, 0, 0)),
        out_shape=jax.ShapeDtypeStruct((DEPTH, 6, N_COND, D), F32),
        compiler_params=_cparams(("parallel", "parallel")),
        name="ada_params",
    )(cond, w_mod, b_mod.reshape(DEPTH, 6, 1, D))
    mods = jnp.transpose(out, (0, 2, 1, 3))
    return jnp.pad(mods, ((0, 0), (0, 0), (0, 2), (0, 0)))


def _entry_kernel(ctx_ref, x_ref, pe_ref, o_ref):
    i = pl.program_id(0)

    @pl.when(i < CTX_TILES)
    def _():
        o_ref[...] = ctx_ref[...]

    @pl.when(i >= CTX_TILES)
    def _():
        o_ref[...] = x_ref[...] + pe_ref[...]


def _entry(ctx2, x2, pe):
    return pl.pallas_call(
        _entry_kernel,
        grid=(N_TILES,),
        in_specs=[pl.BlockSpec((TM, D), lambda i: (jnp.minimum(i, CTX_TILES - 1), 0)),
                  pl.BlockSpec((TM, D), lambda i: (jnp.maximum(i - CTX_TILES, 0), 0)),
                  pl.BlockSpec((TM, D), lambda i: (jnp.maximum(i - CTX_TILES, 0) % TILES_PER_SEQ, 0))],
        out_specs=pl.BlockSpec((TM, D), lambda i: (i, 0)),
        out_shape=jax.ShapeDtypeStruct((NT, D), F32),
        compiler_params=_cparams(("parallel",)),
        name="entry",
    )(ctx2, x2, pe)


def _grid_pos_embed():
    rows = SEQ // GRID_W
    quarter = D // 4
    half = D // 2
    omega = 1.0 / (POS_BASE ** (jnp.arange(quarter, dtype=F32) / quarter))

    def sincos(p):
        ang = p[:, None] * omega[None, :]
        return jnp.concatenate([jnp.sin(ang), jnp.cos(ang)], axis=-1)

    row_e = sincos(jnp.arange(rows, dtype=F32))
    col_e = sincos(jnp.arange(GRID_W, dtype=F32))
    pe = jnp.concatenate([jnp.broadcast_to(row_e[:, None, :], (rows, GRID_W, half)),
                          jnp.broadcast_to(col_e[None, :, :], (rows, GRID_W, half))], axis=-1)
    return pe.reshape(SEQ, D)


def _premix_kernel(x_ref, m_ref, g_ref, w_ref, u_ref, v_ref, xb_ref, gb_ref, ga_ref, gr_ref):
    x = x_ref[...]
    h = _rms(x, g_ref[...]) * (1.0 + m_ref[0, 1:2, :]) + m_ref[0, 0:1, :]
    hb = h.astype(BF16)

    def proj(lo, width):
        return jnp.dot(hb, w_ref[:, lo:lo + width], preferred_element_type=F32)

    u_ref[...] = jax.nn.gelu(proj(0, GW)).astype(BF16)
    v_ref[...] = jax.nn.gelu(proj(GW, GW)).astype(BF16)
    xb_ref[...] = proj(2 * GW, LW)
    gb_ref[...] = jax.nn.gelu(proj(2 * GW + LW, LW)).astype(BF16)
    base = 2 * GW + 2 * LW
    for j in range(2):
        ga_ref[:, j * 512:(j + 1) * 512] = jax.nn.sigmoid(proj(base + j * 512, 512)).astype(BF16)
        gr_ref[:, j * 512:(j + 1) * 512] = jax.nn.sigmoid(proj(base + D + j * 512, 512)).astype(BF16)


def _premix(xtok, mods, g_pre, w_in_bf):
    tile = lambda w: pl.BlockSpec((TM, w), lambda i: (i, 0))
    return pl.pallas_call(
        _premix_kernel,
        grid=(N_TILES,),
        in_specs=[tile(D),
                  pl.BlockSpec((1, 8, D), lambda i: (_mod_row(i), 0, 0)),
                  pl.BlockSpec((1, D), lambda i: (0, 0)),
                  pl.BlockSpec((D, IN_WIDTH), lambda i: (0, 0))],
        out_specs=[tile(GW), tile(GW), tile(LW), tile(LW), tile(D), tile(D)],
        out_shape=[jax.ShapeDtypeStruct((NT, GW), BF16), jax.ShapeDtypeStruct((NT, GW), BF16),
                   jax.ShapeDtypeStruct((NT, LW), F32), jax.ShapeDtypeStruct((NT, LW), BF16),
                   jax.ShapeDtypeStruct((NT, D), BF16), jax.ShapeDtypeStruct((NT, D), BF16)],
        compiler_params=_cparams(("parallel",)),
        name="premix",
    )(xtok, mods, g_pre, w_in_bf)


def _lru_block(b, j, reverse):
    k = (LRU_STEPS - 1 - j) if reverse else (j - 1)
    return jnp.where(j == 0, b, BATCH + b * (SEQ // TL) + k), k


def _lru_kernel(x_ref, xp_ref, xn_ref, cw_ref, cb_ref, wg_ref, bg_ref, lam_ref, h_ref,
                xpad, a_buf, d_buf, carry, *, reverse):
    j = pl.program_id(1)
    k = (LRU_STEPS - 1 - j) if reverse else (j - 1)
    has_prev = jnp.logical_and(j > 0, k >= 1)
    has_next = jnp.logical_and(j > 0, k <= SEQ // TL - 2)

    @pl.when(j == 0)
    def _():
        carry[...] = jnp.zeros_like(carry)

    xpad[0:SUBLANES, :] = jnp.where(has_prev, xp_ref[...], 0.0)
    xpad[SUBLANES:SUBLANES + TL, :] = x_ref[...]
    xpad[SUBLANES + TL:, :] = jnp.where(has_next, xn_ref[...], 0.0)
    xc = cb_ref[...]
    for tap in range(CONV_W):
        off = SUBLANES - CONV_LEFT + tap
        xc = xc + xpad[off:off + TL, :] * cw_ref[tap:tap + 1, :]

    gates = jnp.dot(xc.astype(BF16), wg_ref[0], preferred_element_type=F32) + bg_ref[0]
    r = 0.5 * jnp.tanh(0.5 * gates[:, :LW]) + 0.5
    ig = 0.5 * jnp.tanh(0.5 * gates[:, LW:]) + 0.5
    lam = lam_ref[0]
    softplus_neg = jnp.maximum(-lam, 0.0) + jnp.log1p(jnp.exp(-jnp.abs(lam)))
    log_a = -LRU_C * r * softplus_neg
    a_buf[...] = jnp.exp(log_a)
    th = jnp.tanh(log_a)
    d_buf[...] = jnp.sqrt(-2.0 * th / (1.0 - th)) * (ig * xc)

    row = lax.broadcasted_iota(jnp.int32, (SUBLANES, LW), 0)
    n_grp = TL // SUBLANES

    def body(g, c):
        gi = (n_grp - 1 - g) if reverse else g
        sl = pl.ds(pl.multiple_of(gi * SUBLANES, SUBLANES), SUBLANES)
        a = a_buf[sl, :]
        d = d_buf[sl, :]
        for s in (1, 2, 4):
            shift = (SUBLANES - s) if reverse else s
            m = (row < SUBLANES - s) if reverse else (row >= s)
            d = jnp.where(m, a * pltpu.roll(d, shift, 0) + d, d)
            a = jnp.where(m, a * pltpu.roll(a, shift, 0), a)
        h = d + a * c
        h_ref[sl, :] = h
        edge = 0 if reverse else SUBLANES - 1
        return jnp.broadcast_to(h[edge:edge + 1, :], (SUBLANES, LW))

    carry[...] = lax.fori_loop(0, n_grp, body, carry[...], unroll=4)


def _lru(xb, conv_w, conv_b, wg, bg, lam, reverse):
    n8 = NT // SUBLANES
    per8 = TL // SUBLANES

    def cur(b, j):
        return (_lru_block(b, j, reverse)[0], 0)

    def prev(b, j):
        return (jnp.maximum(_lru_block(b, j, reverse)[0] * per8 - 1, 0), 0)

    def nxt(b, j):
        return (jnp.minimum(_lru_block(b, j, reverse)[0] * per8 + per8, n8 - 1), 0)

    const2 = lambda shape: pl.BlockSpec(shape, lambda b, j: (0, 0))
    const3 = lambda shape: pl.BlockSpec(shape, lambda b, j: (0, 0, 0))
    return pl.pallas_call(
        functools.partial(_lru_kernel, reverse=reverse),
        grid=(BATCH, LRU_STEPS),
        in_specs=[pl.BlockSpec((TL, LW), cur),
                  pl.BlockSpec((SUBLANES, LW), prev),
                  pl.BlockSpec((SUBLANES, LW), nxt),
                  const2((CONV_W, LW)), const2((1, LW)),
                  const3((1, LW, 2 * LW)), const3((1, 1, 2 * LW)), const3((1, 1, LW))],
        out_specs=pl.BlockSpec((TL, LW), cur),
        out_shape=jax.ShapeDtypeStruct((NT, LW), F32),
        scratch_shapes=[pltpu.VMEM((TL + 2 * SUBLANES, LW), F32),
                        pltpu.VMEM((TL, LW), F32), pltpu.VMEM((TL, LW), F32),
                        pltpu.VMEM((SUBLANES, LW), F32)],
        compiler_params=_cparams(("arbitrary", "arbitrary")),
        name="lru_bwd" if reverse else "lru_fwd",
    )(xb, xb, xb, conv_w, conv_b, wg, bg, lam)


def _block_diag(w):
    eye = jnp.eye(HEADS, dtype=w.dtype)
    return jnp.einsum('hij,hg->higj', w, eye).reshape(LW, LW)


def _postmix_kernel(x_ref, m_ref, u_ref, v_ref, gb_ref, ga_ref, gr_ref, hf_ref, hb_ref,
                    lng_ref, lnb_ref, ws_ref, bs_ref, woa_ref, wob_ref, wo_ref, gp_ref, o_ref,
                    mix_buf):
    v = v_ref[...].astype(F32)
    mu = jnp.mean(v, axis=-1, keepdims=True)
    var = jnp.mean(jnp.square(v - mu), axis=-1, keepdims=True)
    vn = ((v - mu) * lax.rsqrt(var + LN_EPS) * lng_ref[...] + lnb_ref[...]).astype(BF16)
    col_grp = lax.broadcasted_iota(jnp.int32, (CHUNK, GW), 1) // (GW // GROUPS)
    for c in range(TM // CHUNK):
        vc = vn[c * CHUNK:(c + 1) * CHUNK, :]
        stacked = jnp.concatenate(
            [jnp.where(col_grp == g, vc, jnp.zeros_like(vc)) for g in range(GROUPS)], axis=0)
        mix_buf[c * CHUNK:(c + 1) * CHUNK, :] = (
            jnp.dot(ws_ref[...], stacked, preferred_element_type=F32) + bs_ref[...])
    ya = (u_ref[...].astype(F32) * mix_buf[...]).astype(BF16)
    za = jnp.dot(ya, woa_ref[...], preferred_element_type=F32)
    yb = ((hf_ref[...] + hb_ref[...]) * gb_ref[...].astype(F32)).astype(BF16)
    zb = jnp.dot(yb, wob_ref[...], preferred_element_type=F32)
    m = (ga_ref[...].astype(F32) * za + gr_ref[...].astype(F32) * zb).astype(BF16)
    y = jnp.dot(m, wo_ref[...], preferred_element_type=F32)
    o_ref[...] = x_ref[...] + m_ref[0, 2:3, :] * _rms(y, gp_ref[...])


def _postmix(xtok, mods, u, v, gb, ga, gr, hf, hb, ln_g, ln_b, ws_cat, bs_full, woa, wob, wo, g_post):
    tile = lambda w: pl.BlockSpec((TM, w), lambda i: (i, 0))
    const = lambda shape: pl.BlockSpec(shape, lambda i: (0, 0))
    return pl.pallas_call(
        _postmix_kernel,
        grid=(N_TILES,),
        in_specs=[tile(D), pl.BlockSpec((1, 8, D), lambda i: (_mod_row(i), 0, 0)),
                  tile(GW), tile(GW), tile(LW), tile(D), tile(D), tile(LW), tile(LW),
                  const((1, GW)), const((1, GW)), const((CHUNK, GROUPS * CHUNK)), const((CHUNK, GW)),
                  const((GW, D)), const((LW, D)), const((D, D)), const((1, D))],
        out_specs=tile(D),
        out_shape=jax.ShapeDtypeStruct((NT, D), F32),
        scratch_shapes=[pltpu.VMEM((TM, GW), F32)],
        compiler_params=_cparams(("parallel",)),
        name="postmix",
    )(xtok, mods, u, v, gb, ga, gr, hf, hb, ln_g, ln_b, ws_cat, bs_full, woa, wob, wo, g_post)


def _route_kernel(x_ref, m_ref, g_ref, rw_ref, rb_ref, tri_ref, low_ref,
                  h_ref, sp_ref, sw_ref, cnt_ref):
    h = _rms(x_ref[...], g_ref[...]) * (1.0 + m_ref[0, 4:5, :]) + m_ref[0, 3:4, :]
    _to_slabs(h_ref, h)

    logits = lax.dot_general(rw_ref[...], h, (((1,), (1,)), ((), ())),
                             precision=lax.Precision.HIGHEST, preferred_element_type=F32)
    s = jax.nn.sigmoid(logits)
    sel = s + rb_ref[...]
    per_grp = N_EXP // N_GRP
    sel3 = sel.reshape(N_GRP, per_grp, TM)
    m1 = jnp.max(sel3, axis=1)
    is_m1 = sel3 == m1[:, None, :]
    n_m1 = jnp.sum(is_m1.astype(F32), axis=1)
    m2 = jnp.max(jnp.where(is_m1, -jnp.inf, sel3), axis=1)
    grp_score = m1 + jnp.where(n_m1 >= 2.0, m1, m2)

    g_iota = lax.broadcasted_iota(jnp.int32, (N_GRP, TM), 0)
    g_rank = jnp.zeros((N_GRP, TM), F32)
    for g in range(N_GRP):
        other = grp_score[g:g + 1, :]
        g_rank = g_rank + jnp.where(other > grp_score, 1.0, 0.0)
        g_rank = g_rank + jnp.where(jnp.logical_and(other == grp_score, g_iota > g), 1.0, 0.0)
    g_keep = jnp.where(g_rank < TOPK_GRP, 1.0, 0.0)
    e_keep = jnp.broadcast_to(g_keep[:, None, :], (N_GRP, per_grp, TM)).reshape(N_EXP, TM)
    selm = jnp.where(e_keep > 0.0, sel, -jnp.inf)

    slabs = [selm[g * per_grp:(g + 1) * per_grp, :] for g in range(N_GRP)]
    row_iota = lax.broadcasted_iota(jnp.int32, (per_grp, TM), 0)
    ranks = [jnp.zeros((per_grp, TM), F32) for _ in range(N_GRP)]
    for e in range(N_EXP):
        ge, re = divmod(e, per_grp)
        other = slabs[ge][re:re + 1, :]
        for g in range(N_GRP):
            wins_ties = jnp.where(other >= slabs[g], 1.0, 0.0)
            strict = jnp.where(other > slabs[g], 1.0, 0.0)
            if g < ge:
                ranks[g] = ranks[g] + strict
            elif g > ge:
                ranks[g] = ranks[g] + wins_ties
            else:
                ranks[g] = ranks[g] + jnp.where(row_iota > re, wins_ties, strict)
    e_rank = jnp.concatenate(ranks, axis=0)
    chosen = e_rank < TOP_K
    sc = jnp.where(chosen, s, 0.0)
    w = sc / jnp.sum(sc, axis=0, keepdims=True) * ROUTE_SCALE

    cum = jnp.dot(jnp.where(chosen, 1.0, 0.0).astype(BF16), tri_ref[...],
                  preferred_element_type=F32)
    tile_cnt = jnp.broadcast_to(cum[:, TM - 1:TM], (N_EXP, LANES))
    cnt_ref[...] = tile_cnt
    tile_off = jnp.dot(low_ref[...], tile_cnt, precision=lax.Precision.HIGHEST,
                       preferred_element_type=F32)[:, 0:1]
    pos = tile_off + cum - 1.0

    for r in range(TOP_K):
        hit = e_rank == float(r)
        sp_ref[r:r + 1, :] = jnp.sum(jnp.where(hit, pos, 0.0), axis=0, keepdims=True).astype(jnp.int32)
        sw_ref[r:r + 1, :] = jnp.sum(jnp.where(hit, w, 0.0), axis=0, keepdims=True)


def _route(xtok, mods, g_pre, rw_t, rb, tri, low):
    const = lambda shape: pl.BlockSpec(shape, lambda i: (0, 0))
    slot = pl.BlockSpec((TOP_K, TM), lambda i: (0, i))
    return pl.pallas_call(
        _route_kernel,
        grid=(N_TILES,),
        in_specs=[pl.BlockSpec((TM, D), lambda i: (i, 0)),
                  pl.BlockSpec((1, 8, D), lambda i: (_mod_row(i), 0, 0)),
                  const((1, D)), const((N_EXP, D)), const((N_EXP, 1)), const((TM, TM)),
                  const((N_EXP, N_EXP))],
        out_specs=[pl.BlockSpec((TM * SLABS, LANES), lambda i: (i, 0)), slot, slot,
                   pl.BlockSpec((N_EXP, LANES), lambda i: (i, 0))],
        out_shape=[jax.ShapeDtypeStruct((NT * SLABS, LANES), F32),
                   jax.ShapeDtypeStruct((TOP_K, NT), jnp.int32),
                   jax.ShapeDtypeStruct((TOP_K, NT), F32),
                   jax.ShapeDtypeStruct((N_TILES * N_EXP, LANES), F32)],
        compiler_params=_cparams(("parallel",)),
        name="route",
    )(xtok, mods, g_pre, rw_t, rb, tri, low)


SEG_BITS = TM.bit_length()


def _start_segments(i, cnt_ref, dst_ref, make_copy):
    def seg(e, off):
        r = cnt_ref[i * N_EXP + e]
        d = dst_ref[i * N_EXP + e]
        for b in range(SEG_BITS):
            n = 1 << b

            @pl.when(((r >> b) & 1) == 1)
            def _():
                o = r & (n - 1)
                make_copy(off + o, d + o, n).start()

        return off + r

    lax.fori_loop(0, N_EXP, seg, 0)


def _wait_segments(make_copy):
    for _ in range(TOP_K):
        make_copy(0, 0, TM).wait()


def _per_slot(i, fn):
    for slot in range(2):
        @pl.when(i % 2 == slot)
        def _():
            fn(slot)


STAGE_ROWS = TOP_K * TM


def _dispatch_kernel(cnt_ref, dst_ref, sp_ref, h_ref, xs_ref, stage, sems):
    i = pl.program_id(0)

    def step(slot):
        base = slot * STAGE_ROWS

        def make_copy(src, dst, n):
            return pltpu.make_async_copy(stage.at[pl.ds(base + src, n)], xs_ref.at[pl.ds(dst, n)],
                                         sems.at[slot])

        @pl.when(i >= 2)
        def _():
            _wait_segments(make_copy)

        slot_rows = stage.at[pl.ds(base, STAGE_ROWS)]

        def place(t, carry):
            slab = h_ref[t]
            for k in range(TOP_K):
                slot_rows[sp_ref[t * TOP_K + k]] = slab
            return carry

        lax.fori_loop(0, TM, place, 0, unroll=2)
        _start_segments(i, cnt_ref, dst_ref, make_copy)

        @pl.when(i == N_TILES - 1)
        def _():
            _wait_segments(make_copy)

    _per_slot(i, step)

    @pl.when(i == N_TILES - 1)
    def _():
        other = (N_TILES - 2) % 2

        def other_copy(src, dst, n):
            return pltpu.make_async_copy(stage.at[pl.ds(other * STAGE_ROWS + src, n)],
                                         xs_ref.at[pl.ds(dst, n)], sems.at[other])

        _wait_segments(other_copy)


def _dispatch(seg_cnt, seg_dst, spos_flat, h3):
    grid_spec = pltpu.PrefetchScalarGridSpec(
        num_scalar_prefetch=2,
        grid=(N_TILES,),
        in_specs=[pl.BlockSpec((TM * TOP_K,), lambda i, c, d: (i,), memory_space=pltpu.SMEM),
                  pl.BlockSpec((TM, SLABS, LANES), lambda i, c, d: (i, 0, 0))],
        out_specs=pl.BlockSpec(memory_space=pl.ANY),
        scratch_shapes=[pltpu.VMEM((2 * STAGE_ROWS, SLABS, LANES), F32),
                        pltpu.SemaphoreType.DMA((2,))],
    )
    return pl.pallas_call(
        _dispatch_kernel,
        grid_spec=grid_spec,
        out_shape=jax.ShapeDtypeStruct((N_ROWS, SLABS, LANES), F32),
        compiler_params=_cparams(("arbitrary",)),
        name="dispatch",
    )(seg_cnt, seg_dst, spos_flat, h3)


def _expert_kernel(ib_ref, ie_ref, lo_ref, hi_ref, ni_ref, x_ref, wg_ref, wu_ref, wd_ref, y_ref,
                   wgu_bf, wd_bf):
    j = pl.program_id(0)

    def swiglu(r0):
        x = _from_slabs(x_ref, MOE_SUB, r0).astype(BF16)
        gu = jnp.dot(x, wgu_bf[...], preferred_element_type=F32)
        a = (jax.nn.silu(gu[:, :EDIM]) * gu[:, EDIM:]).astype(BF16)
        return jnp.dot(a, wd_bf[...], preferred_element_type=F32)

    @pl.when(j < ni_ref[0])
    def _():
        prev = jnp.maximum(j - 1, 0)

        @pl.when(jnp.logical_or(j == 0, ie_ref[j] != ie_ref[prev]))
        def _():
            wgu_bf[:, :EDIM] = wg_ref[0, 0].astype(BF16)
            wgu_bf[:, EDIM:] = wu_ref[0, 0].astype(BF16)
            wd_bf[...] = wd_ref[0, 0].astype(BF16)

        first = jnp.logical_or(j == 0, ib_ref[j] != ib_ref[prev])

        @pl.when(first)
        def _():
            for r0 in range(0, MOE_BLOCK, MOE_SUB):
                _to_slabs(y_ref, swiglu(r0), r0)

        @pl.when(jnp.logical_not(first))
        def _():
            for r0 in range(0, MOE_BLOCK, MOE_SUB):
                row = r0 + lax.broadcasted_iota(jnp.int32, (MOE_SUB, D), 0)
                mine = jnp.logical_and(row >= lo_ref[j], row < hi_ref[j])
                _to_slabs(y_ref, jnp.where(mine, swiglu(r0), _from_slabs(y_ref, MOE_SUB, r0)), r0)


def _experts(layer, item_blk, item_exp, item_lo, item_hi, n_items, xs2, w_gate, w_up, w_down):
    def rows(j, ib, ie, lo, hi, ni):
        return (ib[jnp.minimum(j, ni[0] - 1)], 0)

    def wsel(j, ib, ie, lo, hi, ni):
        return (layer, ie[jnp.minimum(j, ni[0] - 1)], 0, 0)

    grid_spec = pltpu.PrefetchScalarGridSpec(
        num_scalar_prefetch=5,
        grid=(N_ITEMS,),
        in_specs=[pl.BlockSpec((MOE_BLOCK * SLABS, LANES), rows),
                  pl.BlockSpec((1, 1, D, EDIM), wsel),
                  pl.BlockSpec((1, 1, D, EDIM), wsel),
                  pl.BlockSpec((1, 1, EDIM, D), wsel)],
        out_specs=pl.BlockSpec((MOE_BLOCK * SLABS, LANES), rows),
        scratch_shapes=[pltpu.VMEM((D, 2 * EDIM), BF16), pltpu.VMEM((EDIM, D), BF16)],
    )
    return pl.pallas_call(
        _expert_kernel,
        grid_spec=grid_spec,
        out_shape=jax.ShapeDtypeStruct((N_ROWS * SLABS, LANES), F32),
        compiler_params=_cparams(("arbitrary",)),
        name="experts",
    )(item_blk, item_exp, item_lo, item_hi, n_items, xs2, w_gate, w_up, w_down)


def _combine_kernel(cnt_ref, dst_ref, sp_ref, sw_ref, x_ref, m_ref, h_ref, ys_ref, sgu_ref, sd_ref,
                    gp_ref, o_ref, stage, moe_buf, sems):
    i = pl.program_id(0)

    def fetch(slot):
        def make_copy(dst, src, n):
            return pltpu.make_async_copy(ys_ref.at[pl.ds(src, n)],
                                         stage.at[pl.ds(slot * STAGE_ROWS + dst, n)], sems.at[slot])
        return make_copy

    @pl.when(i == 0)
    def _():
        _start_segments(0, cnt_ref, dst_ref, fetch(0))

    @pl.when(i + 1 < N_TILES)
    def _():
        _per_slot(i + 1, lambda slot: _start_segments(i + 1, cnt_ref, dst_ref, fetch(slot)))

    hb = _from_slabs(h_ref, TM).astype(BF16)
    gu = jnp.dot(hb, sgu_ref[...], preferred_element_type=F32)
    a = (jax.nn.silu(gu[:, :SDIM]) * gu[:, SDIM:]).astype(BF16)
    shared = jnp.dot(a, sd_ref[...], preferred_element_type=F32)

    def mix_slot(slot):
        slot_rows = stage.at[pl.ds(slot * STAGE_ROWS, STAGE_ROWS)]
        _wait_segments(fetch(slot))

        def mix(t, carry):
            j = t * TOP_K
            acc = sw_ref[j] * slot_rows[sp_ref[j]]
            for k in range(1, TOP_K):
                acc = acc + sw_ref[j + k] * slot_rows[sp_ref[j + k]]
            moe_buf[pl.ds(pl.multiple_of(t * SLABS, SLABS), SLABS), :] = acc
            return carry

        lax.fori_loop(0, TM, mix, 0, unroll=2)

    _per_slot(i, mix_slot)
    f = shared + _from_slabs(moe_buf, TM)
    o_ref[...] = x_ref[...] + m_ref[0, 5:6, :] * _rms(f, gp_ref[...])


def _combine(seg_cnt, seg_dst, spos, sw, xtok, mods, h2, ys3, sgu, sd, g_post, latent_only):
    const = lambda shape: pl.BlockSpec(shape, lambda i, c, d: (0, 0))
    skip = CTX_TILES if latent_only else 0
    smem = lambda: pl.BlockSpec((TM * TOP_K,), lambda i, c, d: (i,), memory_space=pltpu.SMEM)
    grid_spec = pltpu.PrefetchScalarGridSpec(
        num_scalar_prefetch=2,
        grid=(N_TILES,),
        in_specs=[smem(), smem(),
                  pl.BlockSpec((TM, D), lambda i, c, d: (i, 0)),
                  pl.BlockSpec((1, 8, D), lambda i, c, d: (_mod_row(i), 0, 0)),
                  pl.BlockSpec((TM * SLABS, LANES), lambda i, c, d: (i, 0)),
                  pl.BlockSpec(memory_space=pl.ANY),
                  const((D, 2 * SDIM)), const((SDIM, D)), const((1, D))],
        out_specs=pl.BlockSpec((TM, D), lambda i, c, d: (jnp.maximum(i - skip, 0), 0)),
        scratch_shapes=[pltpu.VMEM((2 * STAGE_ROWS, SLABS, LANES), F32),
                        pltpu.VMEM((TM * SLABS, LANES), F32),
                        pltpu.SemaphoreType.DMA((2,))],
    )
    return pl.pallas_call(
        _combine_kernel,
        grid_spec=grid_spec,
        out_shape=jax.ShapeDtypeStruct((NT - skip * TM, D), F32),
        compiler_params=_cparams(("arbitrary",)),
        name="combine",
    )(seg_cnt, seg_dst, spos, sw, xtok, mods, h2, ys3, sgu, sd, g_post)


def _moe(layer, xtok, mods, g_pre, g_post, router_w, router_bias, e_g, e_u, e_d, s_g, s_u, s_d, tri, low,
         latent_only):
    h2, spos, slot_w, cnt = _route(
        xtok, mods, g_pre, router_w.T, router_bias.reshape(N_EXP, 1), tri, low)
    tile_cnt = cnt[:, 0].astype(jnp.int32).reshape(N_TILES, N_EXP)
    before = jnp.cumsum(tile_cnt, axis=0) - tile_cnt
    counts = jnp.sum(tile_cnt, axis=0)
    cend = jnp.cumsum(counts)
    cstart = cend - counts
    seg_cnt = tile_cnt.reshape(-1)
    seg_dst = (cstart[None, :] + before).reshape(-1)
    blk_start = jnp.arange(N_BLOCKS, dtype=jnp.int32) * MOE_BLOCK
    inner = jnp.logical_and(counts > 0, cstart % MOE_BLOCK != 0)
    pos = jnp.sort(jnp.concatenate([blk_start, jnp.where(inner, cstart, N_ROWS)]))
    n_items = jnp.sum((pos < N_ROWS).astype(jnp.int32)).reshape(1)
    pos = jnp.minimum(pos, N_ROWS - 1)
    item_blk = pos // MOE_BLOCK
    item_exp = jnp.sum((cend[None, :] <= pos[:, None]).astype(jnp.int32), axis=1)
    is_exp = item_exp[:, None] == jnp.arange(N_EXP, dtype=jnp.int32)[None, :]
    item_end = jnp.sum(jnp.where(is_exp, cend[None, :], 0), axis=1)
    item_lo = pos - item_blk * MOE_BLOCK
    item_hi = jnp.minimum(item_end - item_blk * MOE_BLOCK, MOE_BLOCK)

    spos = spos.T.reshape(-1)
    slot_w = slot_w.T.reshape(-1)
    xs3 = _dispatch(seg_cnt, seg_dst, spos, h2.reshape(NT, SLABS, LANES))
    ys2 = _experts(layer, item_blk, item_exp, item_lo, item_hi, n_items,
                   xs3.reshape(N_ROWS * SLABS, LANES), e_g, e_u, e_d)
    sgu = jnp.concatenate([s_g, s_u], axis=-1).astype(BF16)
    return _combine(seg_cnt, seg_dst, spos, slot_w, xtok, mods, h2,
                    ys2.reshape(N_ROWS, SLABS, LANES), sgu, s_d.astype(BF16), g_post, latent_only)


def kernel(x, c, ctx, c_ctx, w_mod, b_mod, g_pre_mix, g_post_mix, g_pre_ffn, g_post_ffn, w_in, gmlp_ln_g, gmlp_ln_b, gmlp_ws, gmlp_bs, conv_w, conv_b, lru_wa, lru_ba, lru_wx, lru_bx, lru_lambda, w_out_a, w_out_b, w_out, router_w, router_bias, exp_w_gate, exp_w_up, exp_w_down, sh_w_gate, sh_w_up, sh_w_down):
    cond = jnp.concatenate([c_ctx[None, :], c, jnp.zeros((N_COND - 1 - BATCH, D), F32)], axis=0)
    mods_all = _ada(cond, w_mod, b_mod)
    xtok = _entry(ctx.reshape(N_CTX, D), x.reshape(BATCH * SEQ, D), _grid_pos_embed())
    tri = (jnp.arange(TM)[:, None] <= jnp.arange(TM)[None, :]).astype(BF16)
    low = (jnp.arange(N_EXP)[:, None] > jnp.arange(N_EXP)[None, :]).astype(F32)

    for l in range(DEPTH):
        mods = mods_all[l]
        row = lambda p: p[l].reshape(1, -1)
        u, v, xb, gb, ga, gr = _premix(xtok, mods, row(g_pre_mix), w_in[l].astype(BF16))

        hs = []
        for d in range(2):
            wg = jnp.concatenate([_block_diag(lru_wa[l, d]), _block_diag(lru_wx[l, d])], axis=1)
            bg = jnp.concatenate([lru_ba[l, d].reshape(1, LW), lru_bx[l, d].reshape(1, LW)], axis=1)
            hs.append(_lru(xb, conv_w[l], conv_b[l].reshape(1, LW), wg.astype(BF16)[None],
                           bg[None], lru_lambda[l, d].reshape(1, 1, LW), reverse=(d == 1)))

        ws_cat = jnp.transpose(gmlp_ws[l], (1, 0, 2)).reshape(CHUNK, GROUPS * CHUNK).astype(BF16)
        bs_full = jnp.repeat(gmlp_bs[l].T, GW // GROUPS, axis=1)
        xtok = _postmix(xtok, mods, u, v, gb, ga, gr, hs[0], hs[1], row(gmlp_ln_g), row(gmlp_ln_b),
                        ws_cat, bs_full, w_out_a[l].astype(BF16), w_out_b[l].astype(BF16),
                        w_out[l].astype(BF16), row(g_post_mix))

        xtok = _moe(l, xtok, mods, row(g_pre_ffn), row(g_post_ffn), router_w[l], router_bias[l],
                    exp_w_gate, exp_w_up, exp_w_down,
                    sh_w_gate[l], sh_w_up[l], sh_w_down[l], tri, low, latent_only=(l == DEPTH - 1))

    return xtok.reshape(BATCH, SEQ, D)
```

```python
import functools

import jax
import jax.numpy as jnp
from jax import lax
from jax.experimental import pallas as pl
from jax.experimental.pallas import tpu as pltpu

F32 = jnp.float32
BF16 = jnp.bfloat16

D = 1024
BATCH = 4
SEQ = 8192
DEPTH = 2
CTX = 256
GRID_W = 64
POS_BASE = 10000.0
CHUNK = 128
GROUPS = 8
GW = 512
LW = 512
HEADS = 8
HDIM = LW // HEADS
CONV_W = 4
CONV_LEFT = 2
LRU_C = 8.0
IN_WIDTH = 2 * GW + 2 * LW + 2 * D
N_EXP = 64
TOP_K = 8
N_GRP = 8
TOPK_GRP = 4
EDIM = 256
SDIM = 256
ROUTE_SCALE = 2.5
RMS_EPS = 1e-6
LN_EPS = 1e-5

SUBLANES = 8
LANES = 128
SLABS = D // LANES
VMEM_LIMIT = 56 * 1024 * 1024

N_CTX = BATCH * CTX
NT = N_CTX + BATCH * SEQ
TM = 512
N_TILES = NT // TM
CTX_TILES = N_CTX // TM
TILES_PER_SEQ = SEQ // TM
TL = CTX
LRU_STEPS = 1 + SEQ // TL
MOE_BLOCK = 512
MOE_SUB = 512
X_RING = 3
N_ROWS = NT * TOP_K
N_BLOCKS = N_ROWS // MOE_BLOCK
N_ITEMS = N_BLOCKS + N_EXP
N_COND = 8


def _cparams(sem):
    return pltpu.CompilerParams(dimension_semantics=sem, vmem_limit_bytes=VMEM_LIMIT)


def _mod_row(i):
    return jnp.where(i < CTX_TILES, 0, 1 + (i - CTX_TILES) // TILES_PER_SEQ)


def _rms(x, g):
    return x * lax.rsqrt(jnp.mean(x * x, axis=-1, keepdims=True) + RMS_EPS) * g


def _to_slabs(ref, val, row0=0):
    rows = val.shape[0]
    for s in range(SLABS):
        ref[pl.ds(row0 * SLABS + s, rows, stride=SLABS), :] = val[:, s * LANES:(s + 1) * LANES]


def _from_slabs(ref, rows, row0=0):
    return jnp.concatenate(
        [ref[pl.ds(row0 * SLABS + s, rows, stride=SLABS), :] for s in range(SLABS)], axis=1)


def _ada_kernel(c_ref, w_ref, b_ref, o_ref):
    a = jax.nn.silu(c_ref[...])
    o_ref[0, 0] = jnp.dot(a, w_ref[0], precision=lax.Precision.HIGHEST,
                          preferred_element_type=F32) + b_ref[0, 0]


def _ada(cond, w_mod, b_mod):
    out = pl.pallas_call(
        _ada_kernel,
        grid=(DEPTH, 6),
        in_specs=[pl.BlockSpec((N_COND, D), lambda l, j: (0, 0)),
                  pl.BlockSpec((1, D, D), lambda l, j: (l, 0, j)),
                  pl.BlockSpec((1, 1, 1, D), lambda l, j: (l, j, 0, 0))],
        out_specs=pl.BlockSpec((1, 1, N_COND, D), lambda l, j: (l, j, 0, 0)),
        out_shape=jax.ShapeDtypeStruct((DEPTH, 6, N_COND, D), F32),
        compiler_params=_cparams(("parallel", "parallel")),
        name="ada_params",
    )(cond, w_mod, b_mod.reshape(DEPTH, 6, 1, D))
    mods = jnp.transpose(out, (0, 2, 1, 3))
    return jnp.pad(mods, ((0, 0), (0, 0), (0, 2), (0, 0)))


def _entry_kernel(ctx_ref, x_ref, pe_ref, o_ref):
    i = pl.program_id(0)

    @pl.when(i < CTX_TILES)
    def _():
        o_ref[...] = ctx_ref[...]

    @pl.when(i >= CTX_TILES)
    def _():
        o_ref[...] = x_ref[...] + pe_ref[...]


def _entry(ctx2, x2, pe):
    return pl.pallas_call(
        _entry_kernel,
        grid=(N_TILES,),
        in_specs=[pl.BlockSpec((TM, D), lambda i: (jnp.minimum(i, CTX_TILES - 1), 0)),
                  pl.BlockSpec((TM, D), lambda i: (jnp.maximum(i - CTX_TILES, 0), 0)),
                  pl.BlockSpec((TM, D), lambda i: (jnp.maximum(i - CTX_TILES, 0) % TILES_PER_SEQ, 0))],
        out_specs=pl.BlockSpec((TM, D), lambda i: (i, 0)),
        out_shape=jax.ShapeDtypeStruct((NT, D), F32),
        compiler_params=_cparams(("parallel",)),
        name="entry",
    )(ctx2, x2, pe)


def _grid_pos_embed():
    rows = SEQ // GRID_W
    quarter = D // 4
    half = D // 2
    omega = 1.0 / (POS_BASE ** (jnp.arange(quarter, dtype=F32) / quarter))

    def sincos(p):
        ang = p[:, None] * omega[None, :]
        return jnp.concatenate([jnp.sin(ang), jnp.cos(ang)], axis=-1)

    row_e = sincos(jnp.arange(rows, dtype=F32))
    col_e = sincos(jnp.arange(GRID_W, dtype=F32))
    pe = jnp.concatenate([jnp.broadcast_to(row_e[:, None, :], (rows, GRID_W, half)),
                          jnp.broadcast_to(col_e[None, :, :], (rows, GRID_W, half))], axis=-1)
    return pe.reshape(SEQ, D)


def _premix_kernel(x_ref, m_ref, g_ref, w_ref, u_ref, v_ref, xb_ref, gb_ref, ga_ref, gr_ref):
    x = x_ref[...]
    h = _rms(x, g_ref[...]) * (1.0 + m_ref[0, 1:2, :]) + m_ref[0, 0:1, :]
    hb = h.astype(BF16)

    def proj(lo, width):
        return jnp.dot(hb, w_ref[:, lo:lo + width], preferred_element_type=F32)

    u_ref[...] = jax.nn.gelu(proj(0, GW)).astype(BF16)
    v_ref[...] = jax.nn.gelu(proj(GW, GW)).astype(BF16)
    xb_ref[...] = proj(2 * GW, LW)
    gb_ref[...] = jax.nn.gelu(proj(2 * GW + LW, LW)).astype(BF16)
    base = 2 * GW + 2 * LW
    for j in range(2):
        ga_ref[:, j * 512:(j + 1) * 512] = jax.nn.sigmoid(proj(base + j * 512, 512)).astype(BF16)
        gr_ref[:, j * 512:(j + 1) * 512] = jax.nn.sigmoid(proj(base + D + j * 512, 512)).astype(BF16)


def _premix(xtok, mods, g_pre, w_in_bf):
    tile = lambda w: pl.BlockSpec((TM, w), lambda i: (i, 0))
    return pl.pallas_call(
        _premix_kernel,
        grid=(N_TILES,),
        in_specs=[tile(D),
                  pl.BlockSpec((1, 8, D), lambda i: (_mod_row(i), 0, 0)),
                  pl.BlockSpec((1, D), lambda i: (0, 0)),
                  pl.BlockSpec((D, IN_WIDTH), lambda i: (0, 0))],
        out_specs=[tile(GW), tile(GW), tile(LW), tile(LW), tile(D), tile(D)],
        out_shape=[jax.ShapeDtypeStruct((NT, GW), BF16), jax.ShapeDtypeStruct((NT, GW), BF16),
                   jax.ShapeDtypeStruct((NT, LW), F32), jax.ShapeDtypeStruct((NT, LW), BF16),
                   jax.ShapeDtypeStruct((NT, D), BF16), jax.ShapeDtypeStruct((NT, D), BF16)],
        compiler_params=_cparams(("parallel",)),
        name="premix",
    )(xtok, mods, g_pre, w_in_bf)


def _lru_block(b, j, reverse):
    k = (LRU_STEPS - 1 - j) if reverse else (j - 1)
    return jnp.where(j == 0, b, BATCH + b * (SEQ // TL) + k), k


def _lru_kernel(x_ref, xp_ref, xn_ref, cw_ref, cb_ref, wg_ref, bg_ref, lam_ref, h_ref,
                xpad, a_buf, d_buf, carry, *, reverse):
    j = pl.program_id(1)
    k = (LRU_STEPS - 1 - j) if reverse else (j - 1)
    has_prev = jnp.logical_and(j > 0, k >= 1)
    has_next = jnp.logical_and(j > 0, k <= SEQ // TL - 2)

    @pl.when(j == 0)
    def _():
        carry[...] = jnp.zeros_like(carry)

    xpad[0:SUBLANES, :] = jnp.where(has_prev, xp_ref[...], 0.0)
    xpad[SUBLANES:SUBLANES + TL, :] = x_ref[...]
    xpad[SUBLANES + TL:, :] = jnp.where(has_next, xn_ref[...], 0.0)
    xc = cb_ref[...]
    for tap in range(CONV_W):
        off = SUBLANES - CONV_LEFT + tap
        xc = xc + xpad[off:off + TL, :] * cw_ref[tap:tap + 1, :]

    gates = jnp.dot(xc.astype(BF16), wg_ref[0], preferred_element_type=F32) + bg_ref[0]
    r = 0.5 * jnp.tanh(0.5 * gates[:, :LW]) + 0.5
    ig = 0.5 * jnp.tanh(0.5 * gates[:, LW:]) + 0.5
    lam = lam_ref[0]
    softplus_neg = jnp.maximum(-lam, 0.0) + jnp.log1p(jnp.exp(-jnp.abs(lam)))
    log_a = -LRU_C * r * softplus_neg
    a_buf[...] = jnp.exp(log_a)
    th = jnp.tanh(log_a)
    d_buf[...] = jnp.sqrt(-2.0 * th / (1.0 - th)) * (ig * xc)

    row = lax.broadcasted_iota(jnp.int32, (SUBLANES, LW), 0)
    n_grp = TL // SUBLANES

    def body(g, c):
        gi = (n_grp - 1 - g) if reverse else g
        sl = pl.ds(pl.multiple_of(gi * SUBLANES, SUBLANES), SUBLANES)
        a = a_buf[sl, :]
        d = d_buf[sl, :]
        for s in (1, 2, 4):
            shift = (SUBLANES - s) if reverse else s
            m = (row < SUBLANES - s) if reverse else (row >= s)
            d = jnp.where(m, a * pltpu.roll(d, shift, 0) + d, d)
            a = jnp.where(m, a * pltpu.roll(a, shift, 0), a)
        h = d + a * c
        h_ref[sl, :] = h
        edge = 0 if reverse else SUBLANES - 1
        return jnp.broadcast_to(h[edge:edge + 1, :], (SUBLANES, LW))

    carry[...] = lax.fori_loop(0, n_grp, body, carry[...], unroll=4)


def _lru(xb, conv_w, conv_b, wg, bg, lam, reverse):
    n8 = NT // SUBLANES
    per8 = TL // SUBLANES

    def cur(b, j):
        return (_lru_block(b, j, reverse)[0], 0)

    def prev(b, j):
        return (jnp.maximum(_lru_block(b, j, reverse)[0] * per8 - 1, 0), 0)

    def nxt(b, j):
        return (jnp.minimum(_lru_block(b, j, reverse)[0] * per8 + per8, n8 - 1), 0)

    const2 = lambda shape: pl.BlockSpec(shape, lambda b, j: (0, 0))
    const3 = lambda shape: pl.BlockSpec(shape, lambda b, j: (0, 0, 0))
    return pl.pallas_call(
        functools.partial(_lru_kernel, reverse=reverse),
        grid=(BATCH, LRU_STEPS),
        in_specs=[pl.BlockSpec((TL, LW), cur),
                  pl.BlockSpec((SUBLANES, LW), prev),
                  pl.BlockSpec((SUBLANES, LW), nxt),
                  const2((CONV_W, LW)), const2((1, LW)),
                  const3((1, LW, 2 * LW)), const3((1, 1, 2 * LW)), const3((1, 1, LW))],
        out_specs=pl.BlockSpec((TL, LW), cur),
        out_shape=jax.ShapeDtypeStruct((NT, LW), F32),
        scratch_shapes=[pltpu.VMEM((TL + 2 * SUBLANES, LW), F32),
                        pltpu.VMEM((TL, LW), F32), pltpu.VMEM((TL, LW), F32),
                        pltpu.VMEM((SUBLANES, LW), F32)],
        compiler_params=_cparams(("arbitrary", "arbitrary")),
        name="lru_bwd" if reverse else "lru_fwd",
    )(xb, xb, xb, conv_w, conv_b, wg, bg, lam)


def _block_diag(w):
    eye = jnp.eye(HEADS, dtype=w.dtype)
    return jnp.einsum('hij,hg->higj', w, eye).reshape(LW, LW)


def _postmix_kernel(x_ref, m_ref, u_ref, v_ref, gb_ref, ga_ref, gr_ref, hf_ref, hb_ref,
                    lng_ref, lnb_ref, ws_ref, bs_ref, woa_ref, wob_ref, wo_ref, gp_ref, o_ref,
                    mix_buf):
    v = v_ref[...].astype(F32)
    mu = jnp.mean(v, axis=-1, keepdims=True)
    var = jnp.mean(jnp.square(v - mu), axis=-1, keepdims=True)
    vn = ((v - mu) * lax.rsqrt(var + LN_EPS) * lng_ref[...] + lnb_ref[...]).astype(BF16)
    col_grp = lax.broadcasted_iota(jnp.int32, (CHUNK, GW), 1) // (GW // GROUPS)
    for c in range(TM // CHUNK):
        vc = vn[c * CHUNK:(c + 1) * CHUNK, :]
        stacked = jnp.concatenate(
            [jnp.where(col_grp == g, vc, jnp.zeros_like(vc)) for g in range(GROUPS)], axis=0)
        mix_buf[c * CHUNK:(c + 1) * CHUNK, :] = (
            jnp.dot(ws_ref[...], stacked, preferred_element_type=F32) + bs_ref[...])
    ya = (u_ref[...].astype(F32) * mix_buf[...]).astype(BF16)
    za = jnp.dot(ya, woa_ref[...], preferred_element_type=F32)
    yb = ((hf_ref[...] + hb_ref[...]) * gb_ref[...].astype(F32)).astype(BF16)
    zb = jnp.dot(yb, wob_ref[...], preferred_element_type=F32)
    m = (ga_ref[...].astype(F32) * za + gr_ref[...].astype(F32) * zb).astype(BF16)
    y = jnp.dot(m, wo_ref[...], preferred_element_type=F32)
    o_ref[...] = x_ref[...] + m_ref[0, 2:3, :] * _rms(y, gp_ref[...])


def _postmix(xtok, mods, u, v, gb, ga, gr, hf, hb, ln_g, ln_b, ws_cat, bs_full, woa, wob, wo, g_post):
    tile = lambda w: pl.BlockSpec((TM, w), lambda i: (i, 0))
    const = lambda shape: pl.BlockSpec(shape, lambda i: (0, 0))
    return pl.pallas_call(
        _postmix_kernel,
        grid=(N_TILES,),
        in_specs=[tile(D), pl.BlockSpec((1, 8, D), lambda i: (_mod_row(i), 0, 0)),
                  tile(GW), tile(GW), tile(LW), tile(D), tile(D), tile(LW), tile(LW),
                  const((1, GW)), const((1, GW)), const((CHUNK, GROUPS * CHUNK)), const((CHUNK, GW)),
                  const((GW, D)), const((LW, D)), const((D, D)), const((1, D))],
        out_specs=tile(D),
        out_shape=jax.ShapeDtypeStruct((NT, D), F32),
        scratch_shapes=[pltpu.VMEM((TM, GW), F32)],
        compiler_params=_cparams(("parallel",)),
        name="postmix",
    )(xtok, mods, u, v, gb, ga, gr, hf, hb, ln_g, ln_b, ws_cat, bs_full, woa, wob, wo, g_post)


def _route_kernel(x_ref, m_ref, g_ref, rw_ref, rb_ref, tri_ref, low_ref,
                  h_ref, sp_ref, sw_ref, cnt_ref):
    h = _rms(x_ref[...], g_ref[...]) * (1.0 + m_ref[0, 4:5, :]) + m_ref[0, 3:4, :]
    _to_slabs(h_ref, h)

    logits = lax.dot_general(rw_ref[...], h, (((1,), (1,)), ((), ())),
                             precision=lax.Precision.HIGHEST, preferred_element_type=F32)
    s = jax.nn.sigmoid(logits)
    sel = s + rb_ref[...]
    per_grp = N_EXP // N_GRP
    sel3 = sel.reshape(N_GRP, per_grp, TM)
    m1 = jnp.max(sel3, axis=1)
    is_m1 = sel3 == m1[:, None, :]
    n_m1 = jnp.sum(is_m1.astype(F32), axis=1)
    m2 = jnp.max(jnp.where(is_m1, -jnp.inf, sel3), axis=1)
    grp_score = m1 + jnp.where(n_m1 >= 2.0, m1, m2)

    g_iota = lax.broadcasted_iota(jnp.int32, (N_GRP, TM), 0)
    g_rank = jnp.zeros((N_GRP, TM), F32)
    for g in range(N_GRP):
        other = grp_score[g:g + 1, :]
        g_rank = g_rank + jnp.where(other > grp_score, 1.0, 0.0)
        g_rank = g_rank + jnp.where(jnp.logical_and(other == grp_score, g_iota > g), 1.0, 0.0)
    g_keep = jnp.where(g_rank < TOPK_GRP, 1.0, 0.0)
    e_keep = jnp.broadcast_to(g_keep[:, None, :], (N_GRP, per_grp, TM)).reshape(N_EXP, TM)
    selm = jnp.where(e_keep > 0.0, sel, -jnp.inf)

    slabs = [selm[g * per_grp:(g + 1) * per_grp, :] for g in range(N_GRP)]
    row_iota = lax.broadcasted_iota(jnp.int32, (per_grp, TM), 0)
    ranks = [jnp.zeros((per_grp, TM), F32) for _ in range(N_GRP)]
    for e in range(N_EXP):
        ge, re = divmod(e, per_grp)
        other = slabs[ge][re:re + 1, :]
        for g in range(N_GRP):
            wins_ties = jnp.where(other >= slabs[g], 1.0, 0.0)
            strict = jnp.where(other > slabs[g], 1.0, 0.0)
            if g < ge:
                ranks[g] = ranks[g] + strict
            elif g > ge:
                ranks[g] = ranks[g] + wins_ties
            else:
                ranks[g] = ranks[g] + jnp.where(row_iota > re, wins_ties, strict)
    e_rank = jnp.concatenate(ranks, axis=0)
    chosen = e_rank < TOP_K
    sc = jnp.where(chosen, s, 0.0)
    w = sc / jnp.sum(sc, axis=0, keepdims=True) * ROUTE_SCALE

    cum = jnp.dot(jnp.where(chosen, 1.0, 0.0).astype(BF16), tri_ref[...],
                  preferred_element_type=F32)
    tile_cnt = jnp.broadcast_to(cum[:, TM - 1:TM], (N_EXP, LANES))
    cnt_ref[...] = tile_cnt
    tile_off = jnp.dot(low_ref[...], tile_cnt, precision=lax.Precision.HIGHEST,
                       preferred_element_type=F32)[:, 0:1]
    pos = tile_off + cum - 1.0

    for r in range(TOP_K):
        hit = e_rank == float(r)
        sp_ref[r:r + 1, :] = jnp.sum(jnp.where(hit, pos, 0.0), axis=0, keepdims=True).astype(jnp.int32)
        sw_ref[r:r + 1, :] = jnp.sum(jnp.where(hit, w, 0.0), axis=0, keepdims=True)


def _route(xtok, mods, g_pre, rw_t, rb, tri, low):
    const = lambda shape: pl.BlockSpec(shape, lambda i: (0, 0))
    slot = pl.BlockSpec((TOP_K, TM), lambda i: (0, i))
    return pl.pallas_call(
        _route_kernel,
        grid=(N_TILES,),
        in_specs=[pl.BlockSpec((TM, D), lambda i: (i, 0)),
                  pl.BlockSpec((1, 8, D), lambda i: (_mod_row(i), 0, 0)),
                  const((1, D)), const((N_EXP, D)), const((N_EXP, 1)), const((TM, TM)),
                  const((N_EXP, N_EXP))],
        out_specs=[pl.BlockSpec((TM * SLABS, LANES), lambda i: (i, 0)), slot, slot,
                   pl.BlockSpec((N_EXP, LANES), lambda i: (i, 0))],
        out_shape=[jax.ShapeDtypeStruct((NT * SLABS, LANES), F32),
                   jax.ShapeDtypeStruct((TOP_K, NT), jnp.int32),
                   jax.ShapeDtypeStruct((TOP_K, NT), F32),
                   jax.ShapeDtypeStruct((N_TILES * N_EXP, LANES), F32)],
        compiler_params=_cparams(("parallel",)),
        name="route",
    )(xtok, mods, g_pre, rw_t, rb, tri, low)


SEG_BITS = TM.bit_length()


def _start_segments(i, cnt_ref, dst_ref, make_copy):
    def seg(e, off):
        r = cnt_ref[i * N_EXP + e]
        d = dst_ref[i * N_EXP + e]
        for b in range(SEG_BITS):
            n = 1 << b

            @pl.when(((r >> b) & 1) == 1)
            def _():
                o = r & (n - 1)
                make_copy(off + o, d + o, n).start()

        return off + r

    lax.fori_loop(0, N_EXP, seg, 0)


def _wait_segments(make_copy):
    for _ in range(TOP_K):
        make_copy(0, 0, TM).wait()


def _per_slot(i, fn):
    for slot in range(2):
        @pl.when(i % 2 == slot)
        def _():
            fn(slot)


STAGE_ROWS = TOP_K * TM


def _dispatch_kernel(cnt_ref, dst_ref, sp_ref, h_ref, xs_ref, stage, sems):
    i = pl.program_id(0)

    def step(slot):
        base = slot * STAGE_ROWS

        def make_copy(src, dst, n):
            return pltpu.make_async_copy(stage.at[pl.ds(base + src, n)], xs_ref.at[pl.ds(dst, n)],
                                         sems.at[slot])

        @pl.when(i >= 2)
        def _():
            _wait_segments(make_copy)

        slot_rows = stage.at[pl.ds(base, STAGE_ROWS)]

        def place(t, carry):
            slab = h_ref[t]
            for k in range(TOP_K):
                slot_rows[sp_ref[t * TOP_K + k]] = slab
            return carry

        lax.fori_loop(0, TM, place, 0, unroll=2)
        _start_segments(i, cnt_ref, dst_ref, make_copy)

        @pl.when(i == N_TILES - 1)
        def _():
            _wait_segments(make_copy)

    _per_slot(i, step)

    @pl.when(i == N_TILES - 1)
    def _():
        other = (N_TILES - 2) % 2

        def other_copy(src, dst, n):
            return pltpu.make_async_copy(stage.at[pl.ds(other * STAGE_ROWS + src, n)],
                                         xs_ref.at[pl.ds(dst, n)], sems.at[other])

        _wait_segments(other_copy)


def _dispatch(seg_cnt, seg_dst, spos_flat, h3):
    grid_spec = pltpu.PrefetchScalarGridSpec(
        num_scalar_prefetch=2,
        grid=(N_TILES,),
        in_specs=[pl.BlockSpec((TM * TOP_K,), lambda i, c, d: (i,), memory_space=pltpu.SMEM),
                  pl.BlockSpec((TM, SLABS, LANES), lambda i, c, d: (i, 0, 0))],
        out_specs=pl.BlockSpec(memory_space=pl.ANY),
        scratch_shapes=[pltpu.VMEM((2 * STAGE_ROWS, SLABS, LANES), F32),
                        pltpu.SemaphoreType.DMA((2,))],
    )
    return pl.pallas_call(
        _dispatch_kernel,
        grid_spec=grid_spec,
        out_shape=jax.ShapeDtypeStruct((N_ROWS, SLABS, LANES), F32),
        compiler_params=_cparams(("arbitrary",)),
        name="dispatch",
    )(seg_cnt, seg_dst, spos_flat, h3)


def _expert_kernel(ib_ref, ie_ref, lo_ref, hi_ref, ni_ref, x_ref, wg_ref, wu_ref, wd_ref, y_ref,
                   wgu_bf, wd_bf, xbuf, xsem):
    j = pl.program_id(0)
    n_items = ni_ref[0]
    blk_rows = MOE_BLOCK * SLABS

    def fetch(item):
        src = x_ref.at[pl.ds(pl.multiple_of(ib_ref[item] * blk_rows, blk_rows), blk_rows)]
        return pltpu.make_async_copy(src, xbuf.at[item % X_RING], xsem.at[item % X_RING])

    @pl.when(j == 0)
    def _():
        for a in range(X_RING - 1):
            @pl.when(a < n_items)
            def _():
                fetch(a).start()

    @pl.when(j + X_RING - 1 < n_items)
    def _():
        fetch(j + X_RING - 1).start()

    @pl.when(j < n_items)
    def _():
        fetch(j).wait()

    x_cur = xbuf.at[j % X_RING]

    def swiglu(r0):
        x = _from_slabs(x_cur, MOE_SUB, r0).astype(BF16)
        gu = jnp.dot(x, wgu_bf[...], preferred_element_type=F32)
        a = (jax.nn.silu(gu[:, :EDIM]) * gu[:, EDIM:]).astype(BF16)
        return jnp.dot(a, wd_bf[...], preferred_element_type=F32)

    @pl.when(j < ni_ref[0])
    def _():
        prev = jnp.maximum(j - 1, 0)

        @pl.when(jnp.logical_or(j == 0, ie_ref[j] != ie_ref[prev]))
        def _():
            wgu_bf[:, :EDIM] = wg_ref[0, 0].astype(BF16)
            wgu_bf[:, EDIM:] = wu_ref[0, 0].astype(BF16)
            wd_bf[...] = wd_ref[0, 0].astype(BF16)

        first = jnp.logical_or(j == 0, ib_ref[j] != ib_ref[prev])

        @pl.when(first)
        def _():
            for r0 in range(0, MOE_BLOCK, MOE_SUB):
                _to_slabs(y_ref, swiglu(r0), r0)

        @pl.when(jnp.logical_not(first))
        def _():
            for r0 in range(0, MOE_BLOCK, MOE_SUB):
                row = r0 + lax.broadcasted_iota(jnp.int32, (MOE_SUB, D), 0)
                mine = jnp.logical_and(row >= lo_ref[j], row < hi_ref[j])
                _to_slabs(y_ref, jnp.where(mine, swiglu(r0), _from_slabs(y_ref, MOE_SUB, r0)), r0)


def _experts(layer, item_blk, item_exp, item_lo, item_hi, n_items, xs2, w_gate, w_up, w_down):
    def rows(j, ib, ie, lo, hi, ni):
        return (ib[jnp.minimum(j, ni[0] - 1)], 0)

    def wsel(j, ib, ie, lo, hi, ni):
        return (layer, ie[jnp.minimum(j, ni[0] - 1)], 0, 0)

    grid_spec = pltpu.PrefetchScalarGridSpec(
        num_scalar_prefetch=5,
        grid=(N_ITEMS,),
        in_specs=[pl.BlockSpec(memory_space=pl.ANY),
                  pl.BlockSpec((1, 1, D, EDIM), wsel),
                  pl.BlockSpec((1, 1, D, EDIM), wsel),
                  pl.BlockSpec((1, 1, EDIM, D), wsel)],
        out_specs=pl.BlockSpec((MOE_BLOCK * SLABS, LANES), rows),
        scratch_shapes=[pltpu.VMEM((D, 2 * EDIM), BF16), pltpu.VMEM((EDIM, D), BF16),
                        pltpu.VMEM((X_RING, MOE_BLOCK * SLABS, LANES), F32),
                        pltpu.SemaphoreType.DMA((X_RING,))],
    )
    return pl.pallas_call(
        _expert_kernel,
        grid_spec=grid_spec,
        out_shape=jax.ShapeDtypeStruct((N_ROWS * SLABS, LANES), F32),
        compiler_params=_cparams(("arbitrary",)),
        name="experts",
    )(item_blk, item_exp, item_lo, item_hi, n_items, xs2, w_gate, w_up, w_down)


def _combine_kernel(cnt_ref, dst_ref, sp_ref, sw_ref, x_ref, m_ref, h_ref, ys_ref, sgu_ref, sd_ref,
                    gp_ref, o_ref, stage, moe_buf, sems):
    i = pl.program_id(0)

    def fetch(slot):
        def make_copy(dst, src, n):
            return pltpu.make_async_copy(ys_ref.at[pl.ds(src, n)],
                                         stage.at[pl.ds(slot * STAGE_ROWS + dst, n)], sems.at[slot])
        return make_copy

    @pl.when(i == 0)
    def _():
        _start_segments(0, cnt_ref, dst_ref, fetch(0))

    @pl.when(i + 1 < N_TILES)
    def _():
        _per_slot(i + 1, lambda slot: _start_segments(i + 1, cnt_ref, dst_ref, fetch(slot)))

    hb = _from_slabs(h_ref, TM).astype(BF16)
    gu = jnp.dot(hb, sgu_ref[...], preferred_element_type=F32)
    a = (jax.nn.silu(gu[:, :SDIM]) * gu[:, SDIM:]).astype(BF16)
    shared = jnp.dot(a, sd_ref[...], preferred_element_type=F32)

    def mix_slot(slot):
        slot_rows = stage.at[pl.ds(slot * STAGE_ROWS, STAGE_ROWS)]
        _wait_segments(fetch(slot))

        def mix(t, carry):
            j = t * TOP_K
            acc = sw_ref[j] * slot_rows[sp_ref[j]]
            for k in range(1, TOP_K):
                acc = acc + sw_ref[j + k] * slot_rows[sp_ref[j + k]]
            moe_buf[pl.ds(pl.multiple_of(t * SLABS, SLABS), SLABS), :] = acc
            return carry

        lax.fori_loop(0, TM, mix, 0, unroll=2)

    _per_slot(i, mix_slot)
    f = shared + _from_slabs(moe_buf, TM)
    o_ref[...] = x_ref[...] + m_ref[0, 5:6, :] * _rms(f, gp_ref[...])


def _combine(seg_cnt, seg_dst, spos, sw, xtok, mods, h2, ys3, sgu, sd, g_post, latent_only):
    const = lambda shape: pl.BlockSpec(shape, lambda i, c, d: (0, 0))
    skip = CTX_TILES if latent_only else 0
    smem = lambda: pl.BlockSpec((TM * TOP_K,), lambda i, c, d: (i,), memory_space=pltpu.SMEM)
    grid_spec = pltpu.PrefetchScalarGridSpec(
        num_scalar_prefetch=2,
        grid=(N_TILES,),
        in_specs=[smem(), smem(),
                  pl.BlockSpec((TM, D), lambda i, c, d: (i, 0)),
                  pl.BlockSpec((1, 8, D), lambda i, c, d: (_mod_row(i), 0, 0)),
                  pl.BlockSpec((TM * SLABS, LANES), lambda i, c, d: (i, 0)),
                  pl.BlockSpec(memory_space=pl.ANY),
                  const((D, 2 * SDIM)), const((SDIM, D)), const((1, D))],
        out_specs=pl.BlockSpec((TM, D), lambda i, c, d: (jnp.maximum(i - skip, 0), 0)),
        scratch_shapes=[pltpu.VMEM((2 * STAGE_ROWS, SLABS, LANES), F32),
                        pltpu.VMEM((TM * SLABS, LANES), F32),
                        pltpu.SemaphoreType.DMA((2,))],
    )
    return pl.pallas_call(
        _combine_kernel,
        grid_spec=grid_spec,
        out_shape=jax.ShapeDtypeStruct((NT - skip * TM, D), F32),
        compiler_params=_cparams(("arbitrary",)),
        name="combine",
    )(seg_cnt, seg_dst, spos, sw, xtok, mods, h2, ys3, sgu, sd, g_post)


def _moe(layer, xtok, mods, g_pre, g_post, router_w, router_bias, e_g, e_u, e_d, s_g, s_u, s_d, tri, low,
         latent_only):
    h2, spos, slot_w, cnt = _route(
        xtok, mods, g_pre, router_w.T, router_bias.reshape(N_EXP, 1), tri, low)
    tile_cnt = cnt[:, 0].astype(jnp.int32).reshape(N_TILES, N_EXP)
    before = jnp.cumsum(tile_cnt, axis=0) - tile_cnt
    counts = jnp.sum(tile_cnt, axis=0)
    cend = jnp.cumsum(counts)
    cstart = cend - counts
    seg_cnt = tile_cnt.reshape(-1)
    seg_dst = (cstart[None, :] + before).reshape(-1)
    blk_start = jnp.arange(N_BLOCKS, dtype=jnp.int32) * MOE_BLOCK
    inner = jnp.logical_and(counts > 0, cstart % MOE_BLOCK != 0)
    pos = jnp.sort(jnp.concatenate([blk_start, jnp.where(inner, cstart, N_ROWS)]))
    n_items = jnp.sum((pos < N_ROWS).astype(jnp.int32)).reshape(1)
    pos = jnp.minimum(pos, N_ROWS - 1)
    item_blk = pos // MOE_BLOCK
    item_exp = jnp.sum((cend[None, :] <= pos[:, None]).astype(jnp.int32), axis=1)
    is_exp = item_exp[:, None] == jnp.arange(N_EXP, dtype=jnp.int32)[None, :]
    item_end = jnp.sum(jnp.where(is_exp, cend[None, :], 0), axis=1)
    item_lo = pos - item_blk * MOE_BLOCK
    item_hi = jnp.minimum(item_end - item_blk * MOE_BLOCK, MOE_BLOCK)

    spos = spos.T.reshape(-1)
    slot_w = slot_w.T.reshape(-1)
    xs3 = _dispatch(seg_cnt, seg_dst, spos, h2.reshape(NT, SLABS, LANES))
    ys2 = _experts(layer, item_blk, item_exp, item_lo, item_hi, n_items,
                   xs3.reshape(N_ROWS * SLABS, LANES), e_g, e_u, e_d)
    sgu = jnp.concatenate([s_g, s_u], axis=-1).astype(BF16)
    return _combine(seg_cnt, seg_dst, spos, slot_w, xtok, mods, h2,
                    ys2.reshape(N_ROWS, SLABS, LANES), sgu, s_d.astype(BF16), g_post, latent_only)


def kernel(x, c, ctx, c_ctx, w_mod, b_mod, g_pre_mix, g_post_mix, g_pre_ffn, g_post_ffn, w_in, gmlp_ln_g, gmlp_ln_b, gmlp_ws, gmlp_bs, conv_w, conv_b, lru_wa, lru_ba, lru_wx, lru_bx, lru_lambda, w_out_a, w_out_b, w_out, router_w, router_bias, exp_w_gate, exp_w_up, exp_w_down, sh_w_gate, sh_w_up, sh_w_down):
    cond = jnp.concatenate([c_ctx[None, :], c, jnp.zeros((N_COND - 1 - BATCH, D), F32)], axis=0)
    mods_all = _ada(cond, w_mod, b_mod)
    xtok = _entry(ctx.reshape(N_CTX, D), x.reshape(BATCH * SEQ, D), _grid_pos_embed())
    tri = (jnp.arange(TM)[:, None] <= jnp.arange(TM)[None, :]).astype(BF16)
    low = (jnp.arange(N_EXP)[:, None] > jnp.arange(N_EXP)[None, :]).astype(F32)

    for l in range(DEPTH):
        mods = mods_all[l]
        row = lambda p: p[l].reshape(1, -1)
        u, v, xb, gb, ga, gr = _premix(xtok, mods, row(g_pre_mix), w_in[l].astype(BF16))

        hs = []
        for d in range(2):
            wg = jnp.concatenate([_block_diag(lru_wa[l, d]), _block_diag(lru_wx[l, d])], axis=1)
            bg = jnp.concatenate([lru_ba[l, d].reshape(1, LW), lru_bx[l, d].reshape(1, LW)], axis=1)
            hs.append(_lru(xb, conv_w[l], conv_b[l].reshape(1, LW), wg.astype(BF16)[None],
                           bg[None], lru_lambda[l, d].reshape(1, 1, LW), reverse=(d == 1)))

        ws_cat = jnp.transpose(gmlp_ws[l], (1, 0, 2)).reshape(CHUNK, GROUPS * CHUNK).astype(BF16)
        bs_full = jnp.repeat(gmlp_bs[l].T, GW // GROUPS, axis=1)
        xtok = _postmix(xtok, mods, u, v, gb, ga, gr, hs[0], hs[1], row(gmlp_ln_g), row(gmlp_ln_b),
                        ws_cat, bs_full, w_out_a[l].astype(BF16), w_out_b[l].astype(BF16),
                        w_out[l].astype(BF16), row(g_post_mix))

        xtok = _moe(l, xtok, mods, row(g_pre_ffn), row(g_post_ffn), router_w[l], router_bias[l],
                    exp_w_gate, exp_w_up, exp_w_down,
                    sh_w_gate[l], sh_w_up[l], sh_w_down[l], tri, low, latent_only=(l == DEPTH - 1))

    return xtok.reshape(BATCH, SEQ, D)
```
